```python
import jax, jax.numpy as jnp
from jax import lax
import numpy as np

D_MODEL = 4096
BATCH = 1
SEQ = 8192
DEPTH = 1
DEC_BATCH = 128
DEC_SEQ = 4
PAST_LEN = 2048
PAGE_SIZE = 128

N_RET_HEADS = 8
RET_DK = 256
RET_DV = 256
RET_CHUNK = 128
ROPE_BASE = 10000.0
N_Q_HEADS = 16
N_KV_HEADS = 4
HEAD_DIM = 128
CMP_LEN = 32
CMP_STRIDE = 16
CMP_HIDDEN = 256
SEL_BLOCK = 64
SEL_TOP = 16
WINDOW = 512
Q_BLOCK = 128
N_KV_KINDS = 4
D_FF = 4 * D_MODEL
EPS = 1e-6

RET_QK = N_RET_HEADS * RET_DK
RET_VW = N_RET_HEADS * RET_DV
NSA_QW = N_Q_HEADS * HEAD_DIM
NSA_KVW = N_KV_HEADS * HEAD_DIM
IN_SPLITS = (RET_QK, RET_QK, RET_VW, RET_VW, NSA_QW, 6 * NSA_KVW, 3 * N_Q_HEADS, D_MODEL, D_MODEL)
D_IN = sum(IN_SPLITS)

kernel_name = 'retnet_nsa_gated_hybrid_step'


def rms_norm(x, g):
    xf = x.astype(jnp.float32)
    y = xf * lax.rsqrt(jnp.mean(xf * xf, axis=-1, keepdims=True) + EPS)
    return (y * g.astype(jnp.float32)).astype(x.dtype)


def head_rms(x):
    xf = x.astype(jnp.float32)
    return xf * lax.rsqrt(jnp.mean(xf * xf, axis=-1, keepdims=True) + EPS)


def masked_softmax(s, mask):
    s = jnp.where(mask, s.astype(jnp.float32), -jnp.inf)
    m = jnp.max(s, axis=-1, keepdims=True)
    m = jnp.where(jnp.isfinite(m), m, 0.0)
    e = jnp.where(mask, jnp.exp(s - m), 0.0)
    return e / jnp.maximum(jnp.sum(e, axis=-1, keepdims=True), 1e-30)


def rotary(x, pos):
    half = x.shape[-1] // 2
    inv = jnp.power(ROPE_BASE, -jnp.arange(half, dtype=jnp.float32) / half)
    ang = pos.astype(jnp.float32)[:, None] * inv[None, :]
    cos = jnp.cos(ang)[None, :, None, :]
    sin = jnp.sin(ang)[None, :, None, :]
    x1, x2 = x[..., :half], x[..., half:]
    return jnp.concatenate([x1 * cos - x2 * sin, x1 * sin + x2 * cos], axis=-1)


def retention_scan(q, k, v, s0):
    B, T = q.shape[:2]
    C = min(RET_CHUNK, T)
    n = T // C
    lg = jnp.log1p(-jnp.exp2(-5.0 - jnp.arange(N_RET_HEADS, dtype=jnp.float32)))
    i = jnp.arange(C, dtype=jnp.float32)
    diff = i[:, None] - i[None, :]
    dmask = jnp.where(diff >= 0, jnp.exp(lg[:, None, None] * jnp.maximum(diff, 0.0)), 0.0)
    q_dec = jnp.exp(lg[None, :] * (i[:, None] + 1.0))[None, :, :, None]
    k_dec = jnp.exp(lg[None, :] * (C - 1.0 - i[:, None]))[None, :, :, None]
    c_dec = jnp.exp(lg * C)[None, :, None, None]

    def to_chunks(a):
        return a.reshape(B, n, C, *a.shape[2:]).swapaxes(0, 1)

    def step(S, blk):
        qc, kc, vc = blk
        inner = jnp.einsum('bihd,bjhd->bhij', qc, kc) * dmask
        o = jnp.einsum('bhij,bjhe->bihe', inner, vc) + jnp.einsum('bihd,bhde->bihe', qc * q_dec, S)
        S = S * c_dec + jnp.einsum('bjhd,bjhe->bhde', kc * k_dec, vc)
        return S, o

    S, o = lax.scan(step, s0, (to_chunks(q), to_chunks(k), to_chunks(v)))
    return o.swapaxes(0, 1).reshape(B, T, N_RET_HEADS, RET_DV), S


def retention_branch(rq, rk, rv, rg, pos, s0):
    B, T, _ = rq.shape
    q = rotary(rq.reshape(B, T, N_RET_HEADS, RET_DK).astype(jnp.float32), pos)
    k = rotary(rk.reshape(B, T, N_RET_HEADS, RET_DK).astype(jnp.float32), pos) * (RET_DK ** -0.5)
    v = rv.reshape(B, T, N_RET_HEADS, RET_DV).astype(jnp.float32)
    o, S = retention_scan(q, k, v, s0.astype(jnp.float32))
    o = head_rms(o).reshape(B, T, RET_VW).astype(rq.dtype)
    return o * jax.nn.silu(rg), S.astype(s0.dtype)


def project(x, norm1_g, w_in, q_norm_g, ks_norm_g, kw_norm_g):
    B, T, _ = x.shape
    points = [int(p) for p in np.cumsum(IN_SPLITS)[:-1]]
    rq, rk, rv, rg, nq, nkv, ng, gr, gn = jnp.split(rms_norm(x, norm1_g) @ w_in, points, axis=-1)
    q = rms_norm(nq.reshape(B, T, N_Q_HEADS, HEAD_DIM), q_norm_g)
    kv = nkv.reshape(B, T, 6, N_KV_HEADS, HEAD_DIM)
    paged = jnp.stack([kv[:, :, 0], kv[:, :, 1], rms_norm(kv[:, :, 2], ks_norm_g), kv[:, :, 3]], axis=2)
    win = jnp.stack([rms_norm(kv[:, :, 4], kw_norm_g), kv[:, :, 5]], axis=2)
    gate = jax.nn.sigmoid(ng.reshape(B, T, N_Q_HEADS, 3).astype(jnp.float32))
    return (rq, rk, rv, rg), (q, gate, paged, win), (gr, gn)


def compress(rows, pe, w1, b1, w2):
    L, G, D = rows.shape
    nc = (L - CMP_LEN) // CMP_STRIDE + 1
    idx = (jnp.arange(nc) * CMP_STRIDE)[:, None] + jnp.arange(CMP_LEN)[None, :]
    blocks = rows[idx] + pe[None, :, None, :]
    flat = blocks.transpose(0, 2, 1, 3).reshape(nc, G, CMP_LEN * D)
    return jax.nn.gelu(flat @ w1 + b1) @ w2


def cmp_ends(L):
    nc = (L - CMP_LEN) // CMP_STRIDE + 1
    return jnp.arange(nc) * CMP_STRIDE + CMP_LEN - 1


def compressed_kv(rows, pe_k, w1_k, b1_k, w2_k, pe_v, w1_v, b1_v, w2_v, kc_norm_g):
    kc = rms_norm(compress(rows[:, 0], pe_k, w1_k, b1_k, w2_k), kc_norm_g)
    vc = compress(rows[:, 1], pe_v, w1_v, b1_v, w2_v)
    return kc, vc


def sel_blocks(rows):
    L = rows.shape[0]
    nsel = -(-L // SEL_BLOCK)
    rows = jnp.pad(rows, ((0, nsel * SEL_BLOCK - L), (0, 0), (0, 0)))
    return rows.reshape(nsel, SEL_BLOCK, *rows.shape[1:])


def cmp_to_sel(nc, nsel):
    cs = jnp.arange(nc)[:, None] * CMP_STRIDE
    bs = jnp.arange(nsel)[None, :] * SEL_BLOCK
    ov = jnp.clip(jnp.minimum(cs + CMP_LEN, bs + SEL_BLOCK) - jnp.maximum(cs, bs), 0, None)
    return ov.astype(jnp.float32) / CMP_LEN


def nsa_core(q, gate, q_pos, kc, vc, kc_end, ksb, vsb, kw, vw, kw_pos):
    Tq = q.shape[0]
    hg = N_Q_HEADS // N_KV_HEADS
    scale = HEAD_DIM ** -0.5
    qg = q.reshape(Tq, N_KV_HEADS, hg, HEAD_DIM)
    qp = q_pos[:, None, None, None]
    s = jnp.einsum('tghd,ngd->tghn', qg, kc) * scale
    p_c = masked_softmax(s, kc_end[None, None, None, :] <= qp)
    o_c = jnp.einsum('tghn,ngd->tghd', p_c, vc)
    nsel = ksb.shape[0]
    imp = jnp.einsum('tgn,ns->tgs', p_c.sum(2), cmp_to_sel(kc.shape[0], nsel))
    blk = jnp.arange(nsel)[None, None, :]
    cur = (q_pos // SEL_BLOCK)[:, None, None]
    forced = (blk == 0) | (blk == cur) | (blk == cur - 1)
    score = jnp.where(forced, jnp.inf, jnp.where(blk <= cur, imp, -jnp.inf))
    n_top = min(SEL_TOP, nsel)
    _, idx = lax.top_k(score, n_top)
    g_ix = jnp.arange(N_KV_HEADS)[None, :, None]
    ks_g = ksb.transpose(2, 0, 1, 3)[g_ix, idx]
    vs_g = vsb.transpose(2, 0, 1, 3)[g_ix, idx]
    kpos = idx[..., None] * SEL_BLOCK + jnp.arange(SEL_BLOCK)
    m_sel = n_top * SEL_BLOCK
    s = jnp.einsum('tghd,tgnkd->tghnk', qg, ks_g).reshape(Tq, N_KV_HEADS, hg, m_sel) * scale
    mask = (kpos <= q_pos[:, None, None, None]).reshape(Tq, N_KV_HEADS, 1, m_sel)
    p_s = masked_softmax(s, mask)
    o_s = jnp.einsum('tghm,tgmd->tghd', p_s, vs_g.reshape(Tq, N_KV_HEADS, m_sel, HEAD_DIM))
    s = jnp.einsum('tghd,lgd->tghl', qg, kw) * scale
    kp = kw_pos[None, None, None, :]
    p_w = masked_softmax(s, (kp <= qp) & (kp > qp - WINDOW) & (kp >= 0))
    o_w = jnp.einsum('tghl,lgd->tghd', p_w, vw)
    o = jnp.stack([o_c, o_s, o_w], axis=-1).reshape(Tq, N_Q_HEADS, HEAD_DIM, 3)
    return jnp.einsum('thdc,thc->thd', o, gate)


def nsa_prompt(q, gate, paged, win, cw):
    B, T = q.shape[:2]
    kc, vc = jax.vmap(lambda r: compressed_kv(r, *cw))(paged)
    kc_end = cmp_ends(T)
    ksb = jax.vmap(sel_blocks)(paged[:, :, 2])
    vsb = jax.vmap(sel_blocks)(paged[:, :, 3])
    win_pad = jnp.pad(win, ((0, 0), (WINDOW, 0), (0, 0), (0, 0), (0, 0)))
    core = jax.vmap(nsa_core, in_axes=(0, 0, None, 0, 0, None, 0, 0, 0, 0, None))

    def block(b):
        s0 = b * Q_BLOCK
        qb = lax.dynamic_slice_in_dim(q, s0, Q_BLOCK, 1)
        gb = lax.dynamic_slice_in_dim(gate, s0, Q_BLOCK, 1)
        wb = lax.dynamic_slice_in_dim(win_pad, s0, WINDOW + Q_BLOCK, 1)
        return core(qb, gb, s0 + jnp.arange(Q_BLOCK), kc, vc, kc_end, ksb, vsb,
                    wb[:, :, 0], wb[:, :, 1], s0 - WINDOW + jnp.arange(WINDOW + Q_BLOCK))

    o = lax.map(block, jnp.arange(T // Q_BLOCK))
    return o.swapaxes(0, 1).reshape(B, T, N_Q_HEADS, HEAD_DIM)


def nsa_sample(q, gate, paged_new, win_all, cache_kv, page_table, cw):
    Tq = q.shape[1]
    past_len = page_table.shape[1] * PAGE_SIZE
    n_win = win_all.shape[1]
    q_pos = past_len + jnp.arange(Tq)
    w_pos = past_len + Tq - n_win + jnp.arange(n_win)
    kc_end = cmp_ends(past_len + Tq)

    def one(args):
        qb, gb, new, pages, wrows = args
        rows = jnp.concatenate([cache_kv[pages].reshape(past_len, *new.shape[1:]), new], axis=0)
        kc, vc = compressed_kv(rows, *cw)
        return nsa_core(qb, gb, q_pos, kc, vc, kc_end, sel_blocks(rows[:, 2]), sel_blocks(rows[:, 3]),
                        wrows[:, 0], wrows[:, 1], w_pos)

    return lax.map(one, (q, gate, paged_new, page_table, win_all))


def merge_mlp(x, y_ret, y_nsa, gr, gn, w_ret_out, w_nsa_out, w_out, norm2_g, w_up, w_down):
    B, T, _ = x.shape
    y_nsa = y_nsa.reshape(B, T, NSA_QW).astype(x.dtype)
    m = jax.nn.sigmoid(gr) * (y_ret @ w_ret_out) + jax.nn.sigmoid(gn) * (y_nsa @ w_nsa_out)
    h = x + m @ w_out
    u = jnp.square(jax.nn.relu(rms_norm(h, norm2_g) @ w_up))
    return h + u @ w_down


def setup_inputs(seed: int = 0) -> dict:
    key = jax.random.key(seed)
    k = jax.random.split(key, 26)
    f32 = jnp.float32
    n_pages = PAST_LEN // PAGE_SIZE
    n_pool = (5 * DEC_BATCH * n_pages + 3) // 4
    win_buf = min(WINDOW, PAST_LEN)

    def dense(kk, shape, fan_in):
        return jax.random.normal(kk, shape, f32) * (fan_in ** -0.5)

    def gain(kk, shape):
        return 1.0 + 0.05 * jax.random.normal(kk, shape, f32)

    def small(kk, shape):
        return 0.02 * jax.random.normal(kk, shape, f32)

    return {
        'x_prompt': jax.random.normal(k[0], (BATCH, SEQ, D_MODEL), f32),
        'x_sample': jax.random.normal(k[1], (DEC_BATCH, DEC_SEQ, D_MODEL), f32),
        'cache_kv': jax.random.normal(k[2], (DEPTH, n_pool, PAGE_SIZE, N_KV_KINDS, N_KV_HEADS, HEAD_DIM), f32),
        'page_table': jax.random.permutation(k[3], n_pool)[:DEC_BATCH * n_pages].reshape(DEC_BATCH, n_pages).astype(jnp.int32),
        'state_win_kv': jax.random.normal(k[4], (DEPTH, DEC_BATCH, win_buf, 2, N_KV_HEADS, HEAD_DIM), f32),
        'state_ret': 0.5 * jax.random.normal(k[5], (DEPTH, DEC_BATCH, N_RET_HEADS, RET_DK, RET_DV), f32),
        'norm1_g': gain(k[6], (DEPTH, D_MODEL)),
        'w_in': dense(k[7], (DEPTH, D_MODEL, D_IN), D_MODEL),
        'w_ret_out': dense(k[8], (DEPTH, RET_VW, D_MODEL), RET_VW),
        'q_norm_g': gain(k[9], (DEPTH, HEAD_DIM)),
        'kc_norm_g': gain(k[10], (DEPTH, HEAD_DIM)),
        'ks_norm_g': gain(k[11], (DEPTH, HEAD_DIM)),
        'kw_norm_g': gain(k[12], (DEPTH, HEAD_DIM)),
        'cmp_pe_k': small(k[13], (DEPTH, CMP_LEN, HEAD_DIM)),
        'cmp_w1_k': dense(k[14], (DEPTH, CMP_LEN * HEAD_DIM, CMP_HIDDEN), CMP_LEN * HEAD_DIM),
        'cmp_b1_k': small(k[15], (DEPTH, CMP_HIDDEN)),
        'cmp_w2_k': dense(k[16], (DEPTH, CMP_HIDDEN, HEAD_DIM), CMP_HIDDEN),
        'cmp_pe_v': small(k[17], (DEPTH, CMP_LEN, HEAD_DIM)),
        'cmp_w1_v': dense(k[18], (DEPTH, CMP_LEN * HEAD_DIM, CMP_HIDDEN), CMP_LEN * HEAD_DIM),
        'cmp_b1_v': small(k[19], (DEPTH, CMP_HIDDEN)),
        'cmp_w2_v': dense(k[20], (DEPTH, CMP_HIDDEN, HEAD_DIM), CMP_HIDDEN),
        'w_nsa_out': dense(k[21], (DEPTH, NSA_QW, D_MODEL), NSA_QW),
        'w_out': dense(k[22], (DEPTH, D_MODEL, D_MODEL), D_MODEL),
        'norm2_g': gain(k[23], (DEPTH, D_MODEL)),
        'w_up': dense(k[24], (DEPTH, D_MODEL, D_FF), D_MODEL),
        'w_down': dense(k[25], (DEPTH, D_FF, D_MODEL), D_FF),
    }


def reference(x_prompt, x_sample, cache_kv, page_table, state_win_kv, state_ret, norm1_g, w_in, w_ret_out,
              q_norm_g, kc_norm_g, ks_norm_g, kw_norm_g, cmp_pe_k, cmp_w1_k, cmp_b1_k, cmp_w2_k,
              cmp_pe_v, cmp_w1_v, cmp_b1_v, cmp_w2_v, w_nsa_out, w_out, norm2_g, w_up, w_down):
    past_len = page_table.shape[1] * PAGE_SIZE
    B, T, _ = x_prompt.shape
    pos_p = jnp.arange(T)
    pos_s = past_len + jnp.arange(x_sample.shape[1])
    hp, hs = x_prompt, x_sample
    kv_p, wn_p, rt_p, kv_s, wn_s, rt_s = [], [], [], [], [], []
    for l in range(DEPTH):
        cw = (cmp_pe_k[l], cmp_w1_k[l], cmp_b1_k[l], cmp_w2_k[l],
              cmp_pe_v[l], cmp_w1_v[l], cmp_b1_v[l], cmp_w2_v[l], kc_norm_g[l])
        pw = (norm1_g[l], w_in[l], q_norm_g[l], ks_norm_g[l], kw_norm_g[l])
        ow = (w_ret_out[l], w_nsa_out[l], w_out[l], norm2_g[l], w_up[l], w_down[l])
        (rq, rk, rv, rg), (q, gate, paged, win), (gr, gn) = project(hp, *pw)
        y_ret, s_new = retention_branch(rq, rk, rv, rg, pos_p,
                                        jnp.zeros((B, N_RET_HEADS, RET_DK, RET_DV), hp.dtype))
        y_nsa = nsa_prompt(q, gate, paged, win, cw)
        kv_p.append(paged)
        wn_p.append(win[:, -min(WINDOW, T):])
        rt_p.append(s_new)
        hp = merge_mlp(hp, y_ret, y_nsa, gr, gn, *ow)
        (rq, rk, rv, rg), (q, gate, paged, win), (gr, gn) = project(hs, *pw)
        y_ret, s_new = retention_branch(rq, rk, rv, rg, pos_s, state_ret[l])
        win_all = jnp.concatenate([state_win_kv[l], win], axis=1)
        y_nsa = nsa_sample(q, gate, paged, win_all, cache_kv[l], page_table, cw)
        kv_s.append(paged)
        wn_s.append(win_all[:, -state_win_kv.shape[2]:])
        rt_s.append(s_new)
        hs = merge_mlp(hs, y_ret, y_nsa, gr, gn, *ow)
    y_prompt = hp
    y_sample = hs
    new_kv_prompt = jnp.stack(kv_p)
    new_win_prompt = jnp.stack(wn_p)
    new_ret_prompt = jnp.stack(rt_p)
    new_kv_sample = jnp.stack(kv_s)
    new_win_sample = jnp.stack(wn_s)
    new_ret_sample = jnp.stack(rt_s)
    return (y_prompt, y_sample, new_kv_prompt, new_win_prompt, new_ret_prompt, new_kv_sample, new_win_sample, new_ret_sample)
```

```python
import functools
import math

import numpy as np
import jax
import jax.numpy as jnp
from jax import lax
from jax.experimental import pallas as pl
from jax.experimental.pallas import tpu as pltpu

N_RET_HEADS = 8
RET_CHUNK = 128
ROPE_BASE = 10000.0
N_Q_HEADS = 16
N_KV_HEADS = 4
CMP_LEN = 32
CMP_STRIDE = 16
SEL_BLOCK = 64
SEL_TOP = 16
WINDOW = 512
Q_BLOCK = 128
EPS = 1e-6

LANES = 128
VMEM_BYTES = 64 * 1024 * 1024
NEG = -1e30

f32 = jnp.float32
bf16 = jnp.bfloat16


def _cparams(sem, est_bytes):
    limit = int(min(VMEM_BYTES - (4 << 20), max(est_bytes + (8 << 20), 32 << 20)))
    return pltpu.CompilerParams(dimension_semantics=sem, vmem_limit_bytes=limit)


def _nbytes(shape, dtype):
    return int(np.prod(shape)) * jnp.dtype(dtype).itemsize


def _rmsnorm_kernel(x_ref, g_ref, o_ref):
    x = x_ref[...]
    y = x * lax.rsqrt(jnp.mean(x * x, axis=-1, keepdims=True) + EPS)
    o_ref[...] = (y * g_ref[...]).astype(o_ref.dtype)


def _rmsnorm(x, g, tm):
    M, D = x.shape
    return pl.pallas_call(
        _rmsnorm_kernel,
        grid=(M // tm,),
        in_specs=[pl.BlockSpec((tm, D), lambda i: (i, 0)),
                  pl.BlockSpec((1, D), lambda i: (0, 0))],
        out_specs=pl.BlockSpec((tm, D), lambda i: (i, 0)),
        out_shape=jax.ShapeDtypeStruct((M, D), bf16),
        compiler_params=_cparams(("parallel",), 2 * tm * D * 6),
        name="rmsnorm",
    )(x, g.reshape(1, D))


def _mm_kernel(*refs, n_pairs, n_extra, nk, epilogue):
    a = refs[:n_pairs]
    b = refs[n_pairs:2 * n_pairs]
    ex = refs[2 * n_pairs:2 * n_pairs + n_extra]
    o_ref = refs[2 * n_pairs + n_extra]
    if nk == 1:
        dots = [jnp.dot(a[i][...], b[i][...], preferred_element_type=f32) for i in range(n_pairs)]
        o_ref[...] = epilogue(dots, [e[...] for e in ex]).astype(o_ref.dtype)
    else:
        acc_ref = refs[-1]
        k = pl.program_id(2)

        @pl.when(k == 0)
        def _():
            acc_ref[...] = jnp.zeros_like(acc_ref)

        acc_ref[...] += jnp.dot(a[0][...], b[0][...], preferred_element_type=f32)

        @pl.when(k == nk - 1)
        def _():
            o_ref[...] = epilogue([acc_ref[...]], [e[...] for e in ex]).astype(o_ref.dtype)


def _mm(pairs, extras, epilogue, out_dtype, tm, tn, tk=None, name="mm"):
    M = pairs[0][0].shape[0]
    N = pairs[0][3].shape[1]
    K0 = pairs[0][2]
    tk = K0 if tk is None else tk
    nk = K0 // tk
    assert M % tm == 0 and N % tn == 0 and K0 % tk == 0
    assert nk == 1 or len(pairs) == 1
    in_specs, args, est = [], [], 0
    for (a, off, K, b) in pairs:
        kb = K if nk == 1 else tk
        assert off % kb == 0
        in_specs.append(pl.BlockSpec((tm, kb), functools.partial(lambda i, j, k, o: (i, o + k), o=off // kb)))
        args.append(a)
        est += 2 * _nbytes((tm, kb), a.dtype)
    for (a, off, K, b) in pairs:
        kb = K if nk == 1 else tk
        in_specs.append(pl.BlockSpec((kb, tn), lambda i, j, k: (k, j)))
        args.append(b)
        est += 2 * _nbytes((kb, tn), b.dtype)
    for (e, off) in extras:
        assert off % tn == 0
        in_specs.append(pl.BlockSpec((tm, tn), functools.partial(lambda i, j, k, o: (i, o + j), o=off // tn)))
        args.append(e)
        est += 2 * _nbytes((tm, tn), e.dtype)
    est += 2 * _nbytes((tm, tn), out_dtype) + (len(pairs) + 2) * tm * tn * 4
    scratch = [pltpu.VMEM((tm, tn), f32)] if nk > 1 else []
    return pl.pallas_call(
        functools.partial(_mm_kernel, n_pairs=len(pairs), n_extra=len(extras), nk=nk, epilogue=epilogue),
        grid=(M // tm, N // tn, nk),
        in_specs=in_specs,
        out_specs=pl.BlockSpec((tm, tn), lambda i, j, k: (i, j)),
        out_shape=jax.ShapeDtypeStruct((M, N), out_dtype),
        scratch_shapes=scratch,
        compiler_params=_cparams(("parallel", "parallel", "arbitrary"), est),
        name=name,
    )(*args)


def _ep_identity(dots, ex):
    return dots[0]


def _ep_merge(dots, ex):
    return jax.nn.sigmoid(ex[0]) * dots[0] + jax.nn.sigmoid(ex[1]) * dots[1]


def _ep_residual(dots, ex):
    return ex[0] + dots[0]


def _ep_relu2(dots, ex):
    return jnp.square(jnp.maximum(dots[0], 0.0))


def _head_norm(x, g):
    return x * lax.rsqrt(jnp.mean(x * x, axis=-1, keepdims=True) + EPS) * g


def _nsa_prep_kernel(q_ref, kv01_ref, kv23_ref, kv45_ref, qg_ref, ksg_ref, kwg_ref,
                     qn_ref, paged_ref, win_ref, kvb_ref, *, n_q, n_kv):
    D = LANES
    kvw = n_kv * D
    for h in range(n_q):
        sl = slice(h * D, (h + 1) * D)
        qn_ref[:, sl] = _head_norm(q_ref[:, sl], qg_ref[...]).astype(qn_ref.dtype)
    paged_ref[:, 0:2 * kvw] = kv01_ref[...]
    for g in range(n_kv):
        sl = slice(g * D, (g + 1) * D)
        ks = _head_norm(kv23_ref[:, sl], ksg_ref[...])
        paged_ref[:, 2 * kvw + g * D:2 * kvw + (g + 1) * D] = ks
        kvb_ref[:, sl] = ks.astype(kvb_ref.dtype)
        kw = _head_norm(kv45_ref[:, sl], kwg_ref[...])
        win_ref[:, sl] = kw
        kvb_ref[:, 2 * kvw + g * D:2 * kvw + (g + 1) * D] = kw.astype(kvb_ref.dtype)
    vs = kv23_ref[:, kvw:2 * kvw]
    paged_ref[:, 3 * kvw:4 * kvw] = vs
    kvb_ref[:, kvw:2 * kvw] = vs.astype(kvb_ref.dtype)
    vw = kv45_ref[:, kvw:2 * kvw]
    win_ref[:, kvw:2 * kvw] = vw
    kvb_ref[:, 3 * kvw:4 * kvw] = vw.astype(kvb_ref.dtype)


def _nsa_prep(proj, lay, q_norm_g, ks_norm_g, kw_norm_g, tm):
    M = proj.shape[0]
    n_q, n_kv, D = N_Q_HEADS, N_KV_HEADS, LANES
    qw, kvw = n_q * D, n_kv * D
    assert lay["nq"] % qw == 0 and lay["nkv"] % (2 * kvw) == 0
    kvb0 = lay["nkv"] // (2 * kvw)
    gspec = pl.BlockSpec((1, D), lambda i: (0, 0))
    return pl.pallas_call(
        functools.partial(_nsa_prep_kernel, n_q=n_q, n_kv=n_kv),
        grid=(M // tm,),
        in_specs=[pl.BlockSpec((tm, qw), functools.partial(lambda i, o: (i, o), o=lay["nq"] // qw)),
                  pl.BlockSpec((tm, 2 * kvw), functools.partial(lambda i, o: (i, o), o=kvb0)),
                  pl.BlockSpec((tm, 2 * kvw), functools.partial(lambda i, o: (i, o), o=kvb0 + 1)),
                  pl.BlockSpec((tm, 2 * kvw), functools.partial(lambda i, o: (i, o), o=kvb0 + 2)),
                  gspec, gspec, gspec],
        out_specs=[pl.BlockSpec((tm, qw), lambda i: (i, 0)),
                   pl.BlockSpec((tm, 4 * kvw), lambda i: (i, 0)),
                   pl.BlockSpec((tm, 2 * kvw), lambda i: (i, 0)),
                   pl.BlockSpec((tm, 4 * kvw), lambda i: (i, 0))],
        out_shape=[jax.ShapeDtypeStruct((M, qw), bf16),
                   jax.ShapeDtypeStruct((M, 4 * kvw), f32),
                   jax.ShapeDtypeStruct((M, 2 * kvw), f32),
                   jax.ShapeDtypeStruct((M, 4 * kvw), bf16)],
        compiler_params=_cparams(("parallel",), 2 * tm * (qw * 6 + kvw * 6 * 4 + kvw * 6 * 4 + kvw * 8)),
        name="nsa_prep",
    )(proj, proj, proj, proj, q_norm_g.reshape(1, D), ks_norm_g.reshape(1, D), kw_norm_g.reshape(1, D))


def _gelu_tanh(x):
    return 0.5 * x * (1.0 + jnp.tanh(math.sqrt(2.0 / math.pi) * (x + 0.044715 * (x * x * x))))


def _compress_rows(H, w1t, w1b, pe, b1, w2):
    half = w1t.shape[0]
    P = jnp.dot(H, w1t, preferred_element_type=f32)
    Q = jnp.dot(H, w1b, preferred_element_type=f32)
    bias = (jnp.dot(pe[:, :half], w1t, preferred_element_type=f32)
            + jnp.dot(pe[:, half:], w1b, preferred_element_type=f32))[0:1] + b1
    hidden = P + pltpu.roll(Q, Q.shape[0] - 1, 0) + bias
    return jnp.dot(_gelu_tanh(hidden).astype(bf16), w2, preferred_element_type=f32)


def _compress_kernel(h_ref, w1t_ref, w1b_ref, pe_ref, b1_ref, w2_ref, g_ref, o_ref, *, normalize):
    out = _compress_rows(h_ref[0], w1t_ref[...], w1b_ref[...], pe_ref[...], b1_ref[...], w2_ref[...])
    if normalize:
        out = _head_norm(out, g_ref[...])
    o_ref[0] = out.astype(o_ref.dtype)


def _cmp_weights(pe, w1, b1, w2):
    half = w1.shape[0] // 2
    pe8 = jnp.broadcast_to(pe.reshape(1, -1), (8, pe.size)).astype(bf16)
    return (w1[:half].astype(bf16), w1[half:].astype(bf16), pe8, b1.reshape(1, -1).astype(f32), w2.astype(bf16))


def _compress_prompt(H, cwk, g, normalize):
    G, n, W = H.shape
    w1t, w1b, pe8, b1, w2 = cwk
    hid, D = w2.shape
    full = lambda a: pl.BlockSpec(a.shape, lambda i: (0,) * a.ndim)
    g2 = g.reshape(1, D)
    return pl.pallas_call(
        functools.partial(_compress_kernel, normalize=normalize),
        grid=(G,),
        in_specs=[pl.BlockSpec((1, n, W), lambda i: (i, 0, 0)), full(w1t), full(w1b), full(pe8), full(b1), full(w2),
                  full(g2)],
        out_specs=pl.BlockSpec((1, n, D), lambda i: (i, 0, 0)),
        out_shape=jax.ShapeDtypeStruct((G, n, D), bf16),
        compiler_params=_cparams(("parallel",), 2 * (n * W * 2 + 2 * W * hid * 2) + 6 * n * hid * 4),
        name="compress_prompt",
    )(H, w1t, w1b, pe8, b1, w2, g2)


def _dot_nt(a, b):
    return lax.dot_general(a, b, (((1,), (1,)), ((), ())), preferred_element_type=f32)


def _dot_split(a, b):
    hi = a.astype(bf16)
    r1 = a - hi.astype(f32)
    mid = r1.astype(bf16)
    lo = (r1 - mid.astype(f32)).astype(bf16)
    return (jnp.dot(hi, b, preferred_element_type=f32) + jnp.dot(mid, b, preferred_element_type=f32)
            + jnp.dot(lo, b, preferred_element_type=f32))


def _softmax_masked(s, maskf):
    vis = maskf > 0.0
    sm = jnp.where(vis, s, NEG)
    m = jnp.max(sm, axis=-1, keepdims=True)
    e = jnp.where(vis, jnp.exp(sm - m), 0.0)
    return e / jnp.maximum(jnp.sum(e, axis=-1, keepdims=True), 1e-30)


def _select_blocks(imp, cur, n_top):
    r, nb = imp.shape
    blk = lax.broadcasted_iota(jnp.int32, (r, nb), 1).astype(f32)
    forced = jnp.where(blk == 0.0, 1.0, 0.0) + jnp.where(blk == cur, 1.0, 0.0) + jnp.where(blk == cur - 1.0, 1.0, 0.0)
    score = jnp.where(forced > 0.0, jnp.inf, jnp.where(blk <= cur, imp, -jnp.inf))
    sel = jnp.zeros((r, nb), f32)
    for _ in range(n_top):
        mx = jnp.max(score, axis=-1, keepdims=True)
        first = jnp.min(jnp.where(score == mx, blk, float(nb)), axis=-1, keepdims=True)
        hit = blk == first
        sel = jnp.where(hit, 1.0, sel)
        score = jnp.where(hit, -jnp.inf, score)
    return sel


def _nsa_prompt_kernel(q_ref, gate_ref, kc_ref, vc_ref, ks_ref, vs_ref, kw_ref, vw_ref, c2s_ref, o_ref,
                       *, hg, n_top, tk):
    D = LANES
    QB = q_ref.shape[0]
    T = ks_ref.shape[0]
    nc = kc_ref.shape[1]
    nb = c2s_ref.shape[1]
    R = hg * QB
    scale = D ** -0.5
    qb = pl.program_id(1)
    s0 = qb * QB
    q = jnp.concatenate([q_ref[:, h * D:(h + 1) * D] for h in range(hg)], axis=0)
    t_row = s0 + (lax.broadcasted_iota(jnp.int32, (R, 1), 0) & (QB - 1))

    kc_end = lax.broadcasted_iota(jnp.int32, (1, nc), 1) * CMP_STRIDE + (CMP_LEN - 1)
    s = _dot_nt(q, kc_ref[0]) * scale
    p_c = _softmax_masked(s, jnp.where(kc_end <= t_row, 1.0, 0.0))
    o_c = jnp.dot(p_c.astype(bf16), vc_ref[0], preferred_element_type=f32)

    p_sum = p_c[0:QB]
    for h in range(1, hg):
        p_sum = p_sum + p_c[h * QB:(h + 1) * QB]
    imp = _dot_split(p_sum, c2s_ref[...])
    t_q = s0 + lax.broadcasted_iota(jnp.int32, (QB, 1), 0)
    cur = (t_q // SEL_BLOCK).astype(f32)
    sel = _select_blocks(imp, cur, n_top).astype(bf16)

    bpt = tk // SEL_BLOCK
    col_blk = lax.broadcasted_iota(jnp.int32, (nb, tk), 1) // SEL_BLOCK
    row_blk = lax.broadcasted_iota(jnp.int32, (nb, tk), 0)
    col_pos = lax.broadcasted_iota(jnp.int32, (1, tk), 1)

    def sel_tile(kt, carry):
        m, l, acc = carry
        k0 = pl.multiple_of(kt * tk, tk)
        expand = jnp.where(row_blk - kt * bpt == col_blk, 1.0, 0.0).astype(bf16)
        chosen = jnp.dot(sel, expand, preferred_element_type=f32)
        chosen = jnp.concatenate([chosen] * hg, axis=0)
        vis = jnp.where(col_pos + k0 <= t_row, chosen, 0.0) > 0.0
        sc = jnp.where(vis, _dot_nt(q, ks_ref[pl.ds(k0, tk), :]) * scale, NEG)
        m_new = jnp.maximum(m, jnp.max(sc, axis=-1, keepdims=True))
        alpha = jnp.exp(m - m_new)
        e = jnp.where(vis, jnp.exp(sc - m_new), 0.0)
        l = alpha * l + jnp.sum(e, axis=-1, keepdims=True)
        acc = alpha * acc + jnp.dot(e.astype(bf16), vs_ref[pl.ds(k0, tk), :], preferred_element_type=f32)
        return m_new, l, acc

    n_tiles = (s0 + QB - 1) // tk + 1
    m0 = jnp.full((R, 1), NEG, f32)
    _, l, acc = lax.fori_loop(0, n_tiles, sel_tile, (m0, jnp.zeros((R, 1), f32), jnp.zeros((R, D), f32)))
    o_s = acc / jnp.maximum(l, 1e-30)

    wl = min(WINDOW + QB, T)
    w0 = pl.multiple_of(jnp.clip(s0 - WINDOW, 0, T - wl), QB)
    kp = w0 + lax.broadcasted_iota(jnp.int32, (1, wl), 1)
    visw = jnp.where(kp <= t_row, 1.0, 0.0) * jnp.where(kp > t_row - WINDOW, 1.0, 0.0)
    s = _dot_nt(q, kw_ref[pl.ds(w0, wl), :]) * scale
    p_w = _softmax_masked(s, visw)
    o_w = jnp.dot(p_w.astype(bf16), vw_ref[pl.ds(w0, wl), :], preferred_element_type=f32)

    gate = jax.nn.sigmoid(gate_ref[...])
    for h in range(hg):
        rows = slice(h * QB, (h + 1) * QB)
        o = (o_c[rows] * gate[:, 3 * h:3 * h + 1] + o_s[rows] * gate[:, 3 * h + 1:3 * h + 2]
             + o_w[rows] * gate[:, 3 * h + 2:3 * h + 3])
        o_ref[:, h * D:(h + 1) * D] = o.astype(o_ref.dtype)


def _cmp_to_sel(nc_pad, nsel_pad, nc, nsel):
    cs = np.arange(nc_pad)[:, None] * CMP_STRIDE
    bs = np.arange(nsel_pad)[None, :] * SEL_BLOCK
    ov = np.clip(np.minimum(cs + CMP_LEN, bs + SEL_BLOCK) - np.maximum(cs, bs), 0, None).astype(np.float32) / CMP_LEN
    ov[nc:, :] = 0.0
    ov[:, nsel:] = 0.0
    return jnp.asarray(ov, dtype=bf16)


def _nsa_prompt(qn, proj, lay, kc, vc, kvb):
    T = qn.shape[0]
    G, D = N_KV_HEADS, LANES
    hg = N_Q_HEADS // G
    QB = Q_BLOCK
    nc_pad = kc.shape[1]
    nc = (T - CMP_LEN) // CMP_STRIDE + 1
    nsel = -(-T // SEL_BLOCK)
    nb = -(-nsel // LANES) * LANES
    c2s = _cmp_to_sel(nc_pad, nb, nc, nsel)
    tk = min(512, T)
    assert T % tk == 0 and T % QB == 0 and lay["gate"] % LANES == 0
    gate_b = lay["gate"] // LANES
    est = 2 * 4 * T * D * 2 + 12 * hg * QB * max(tk, nc_pad, WINDOW + QB) * 4
    return pl.pallas_call(
        functools.partial(_nsa_prompt_kernel, hg=hg, n_top=min(SEL_TOP, nsel), tk=tk),
        grid=(G, T // QB),
        in_specs=[pl.BlockSpec((QB, hg * D), lambda g, i: (i, g)),
                  pl.BlockSpec((QB, LANES), functools.partial(lambda g, i, o: (i, o + g), o=gate_b)),
                  pl.BlockSpec((1, nc_pad, D), lambda g, i: (g, 0, 0)),
                  pl.BlockSpec((1, nc_pad, D), lambda g, i: (g, 0, 0)),
                  pl.BlockSpec((T, D), lambda g, i: (0, g)),
                  pl.BlockSpec((T, D), functools.partial(lambda g, i, o: (0, o + g), o=G)),
                  pl.BlockSpec((T, D), functools.partial(lambda g, i, o: (0, o + g), o=2 * G)),
                  pl.BlockSpec((T, D), functools.partial(lambda g, i, o: (0, o + g), o=3 * G)),
                  pl.BlockSpec((nc_pad, nb), lambda g, i: (0, 0))],
        out_specs=pl.BlockSpec((QB, hg * D), lambda g, i: (i, g)),
        out_shape=jax.ShapeDtypeStruct((T, N_Q_HEADS * D), bf16),
        compiler_params=_cparams(("parallel", "parallel"), est),
        name="nsa_prompt",
    )(qn, proj, kc, vc, kvb, kvb, kvb, kvb, c2s)


def _ret_tables(C, n_valid):
    lg = np.log1p(-np.exp2(-5.0 - np.arange(N_RET_HEADS, dtype=np.float64)))
    i = np.arange(C, dtype=np.float64)
    diff = i[:, None] - i[None, :]
    valid = (i < n_valid)
    dmask = np.where(diff >= 0, np.exp(lg[:, None, None] * np.maximum(diff, 0.0)), 0.0)
    dmask = dmask * (valid[:, None] & valid[None, :])
    q_dec = np.exp(lg[:, None] * (i[None, :] + 1.0)) * valid[None, :]
    k_dec = np.exp(lg[:, None] * (n_valid - 1.0 - i[None, :])) * valid[None, :]
    c_dec = np.exp(lg * n_valid)
    as32 = lambda a: jnp.asarray(a, dtype=f32)
    return (as32(dmask), as32(q_dec[:, :, None]), as32(k_dec[:, :, None]),
            as32(np.broadcast_to(c_dec[:, None, None], (N_RET_HEADS, 1, LANES))))


def _rope_tables(pos, half):
    inv = jnp.power(ROPE_BASE, -jnp.arange(half, dtype=f32) / half)
    ang = pos.astype(f32)[:, None] * inv[None, :]
    return jnp.cos(ang), jnp.sin(ang)


def _rotary(x, cos, sin):
    half = cos.shape[-1]
    x1, x2 = x[:, :half], x[:, half:]
    return jnp.concatenate([x1 * cos - x2 * sin, x1 * sin + x2 * cos], axis=-1)


def _ret_chunk(q_raw, k_raw, v, rg, cos, sin, dmask, q_dec, k_dec, c_dec, S):
    dk = q_raw.shape[-1]
    q = _rotary(q_raw, cos, sin)
    k = _rotary(k_raw, cos, sin) * (dk ** -0.5)
    vb = v.astype(bf16)
    inner = _dot_nt(q.astype(bf16), k.astype(bf16)) * dmask
    o = (jnp.dot(inner.astype(bf16), vb, preferred_element_type=f32)
         + jnp.dot((q * q_dec).astype(bf16), S.astype(bf16), preferred_element_type=f32))
    S_new = S * c_dec + lax.dot_general((k * k_dec).astype(bf16), vb, (((0,), (0,)), ((), ())),
                                        preferred_element_type=f32)
    y = o * lax.rsqrt(jnp.mean(o * o, axis=-1, keepdims=True) + EPS)
    return y * (rg * jax.nn.sigmoid(rg)), S_new


def _ret_prompt_kernel(q_ref, k_ref, v_ref, g_ref, cos_ref, sin_ref, dm_ref, qd_ref, kd_ref, cd_ref,
                       y_ref, s_out_ref, s_ref):
    c = pl.program_id(1)

    @pl.when(c == 0)
    def _():
        s_ref[...] = jnp.zeros_like(s_ref)

    y, S_new = _ret_chunk(q_ref[...], k_ref[...], v_ref[...], g_ref[...], cos_ref[...], sin_ref[...],
                          dm_ref[0], qd_ref[0], kd_ref[0], cd_ref[0][:, 0:1], s_ref[...])
    y_ref[...] = y.astype(y_ref.dtype)
    s_ref[...] = S_new

    @pl.when(c == pl.num_programs(1) - 1)
    def _():
        s_out_ref[0] = S_new


def _ret_prompt(proj, lay, T):
    H = N_RET_HEADS
    dk = lay["ret_dk"]
    C = min(RET_CHUNK, T)
    assert T % C == 0 and lay["rq"] % dk == 0
    cos, sin = _rope_tables(jnp.arange(T), dk // 2)
    dmask, q_dec, k_dec, c_dec = _ret_tables(C, C)
    col = lambda name: functools.partial(lambda h, c, o: (c, o + h), o=lay[name] // dk)
    tab = lambda shp: pl.BlockSpec((1,) + shp, lambda h, c: (h, 0, 0))
    return pl.pallas_call(
        _ret_prompt_kernel,
        grid=(H, T // C),
        in_specs=[pl.BlockSpec((C, dk), col("rq")), pl.BlockSpec((C, dk), col("rk")),
                  pl.BlockSpec((C, dk), col("rv")), pl.BlockSpec((C, dk), col("rg")),
                  pl.BlockSpec((C, dk // 2), lambda h, c: (c, 0)), pl.BlockSpec((C, dk // 2), lambda h, c: (c, 0)),
                  tab((C, C)), tab((C, 1)), tab((C, 1)), tab((1, LANES))],
        out_specs=[pl.BlockSpec((C, dk), lambda h, c: (c, h)),
                   pl.BlockSpec((1, dk, dk), lambda h, c: (h, 0, 0))],
        out_shape=[jax.ShapeDtypeStruct((T, H * dk), bf16),
                   jax.ShapeDtypeStruct((H, dk, dk), f32)],
        scratch_shapes=[pltpu.VMEM((dk, dk), f32)],
        compiler_params=_cparams(("parallel", "arbitrary"), 16 * C * dk * 4 + 4 * dk * dk * 4),
        name="ret_prompt",
    )(proj, proj, proj, proj, cos, sin, dmask, q_dec, k_dec, c_dec)


def _ret_sample_kernel(q_ref, k_ref, v_ref, g_ref, cos_ref, sin_ref, dm_ref, qd_ref, kd_ref, cd_ref, s0_ref,
                       y_ref, s_out_ref, *, n_heads, dk):
    for h in range(n_heads):
        sl = slice(h * dk, (h + 1) * dk)
        y, S_new = _ret_chunk(q_ref[0][:, sl], k_ref[0][:, sl], v_ref[0][:, sl], g_ref[0][:, sl],
                              cos_ref[...], sin_ref[...], dm_ref[h], qd_ref[h], kd_ref[h], cd_ref[h][:, 0:1],
                              s0_ref[0, h])
        y_ref[0, :, sl] = y.astype(y_ref.dtype)
        s_out_ref[0, h] = S_new


def _ret_sample(proj8, lay, state, past_len, n_tok):
    B, P, _ = proj8.shape
    H = N_RET_HEADS
    dk = lay["ret_dk"]
    w = H * dk
    assert all(lay[n] % w == 0 for n in ("rq", "rk", "rv", "rg"))
    cos, sin = _rope_tables(past_len + jnp.arange(P), dk // 2)
    dmask, q_dec, k_dec, c_dec = _ret_tables(P, n_tok)
    col = lambda name: functools.partial(lambda b, o: (b, 0, o), o=lay[name] // w)
    full = lambda a: pl.BlockSpec(a.shape, lambda b: (0,) * a.ndim)
    return pl.pallas_call(
        functools.partial(_ret_sample_kernel, n_heads=H, dk=dk),
        grid=(B,),
        in_specs=[pl.BlockSpec((1, P, w), col("rq")), pl.BlockSpec((1, P, w), col("rk")),
                  pl.BlockSpec((1, P, w), col("rv")), pl.BlockSpec((1, P, w), col("rg")),
                  full(cos), full(sin), full(dmask), full(q_dec), full(k_dec), full(c_dec),
                  pl.BlockSpec((1, H, dk, dk), lambda b: (b, 0, 0, 0))],
        out_specs=[pl.BlockSpec((1, P, w), lambda b: (b, 0, 0)),
                   pl.BlockSpec((1, H, dk, dk), lambda b: (b, 0, 0, 0))],
        out_shape=[jax.ShapeDtypeStruct((B, P, w), bf16),
                   jax.ShapeDtypeStruct((B, H, dk, dk), f32)],
        compiler_params=_cparams(("parallel",), 4 * H * dk * dk * 4 + 16 * P * w * 4),
        name="ret_sample",
    )(proj8, proj8, proj8, proj8, cos, sin, dmask, q_dec, k_dec, c_dec, state)


def _page_half_blocks(page_refs, col0, slab_ref, G):
    D = LANES
    PR = page_refs[0].shape[1]
    for g in range(G):
        for p, r in enumerate(page_refs):
            slab_ref[pl.ds((g * len(page_refs) + p) * PR, PR), :] = r[0, :, col0 + g * D:col0 + (g + 1) * D]
    n = slab_ref.shape[0] // CMP_STRIDE
    return jnp.concatenate([slab_ref[pl.ds(j, n, stride=CMP_STRIDE), :] for j in range(CMP_STRIDE)], axis=1)


def _sample_cmp_kernel(pt_ref, *refs, n_pages, G, hg, n_tok, n_top, past_len, nc):
    D = LANES
    page_refs = refs[:n_pages]
    (q_ref, w1tk, w1bk, pek, b1k, w2k, gk, w1tv, w1bv, pev, b1v, w2v, c2s_ref, rep_ref,
     oc_ref, sel_ref, slab_ref) = refs[n_pages:]
    kvw = G * D
    Hk = _page_half_blocks(page_refs, 0, slab_ref, G).astype(bf16)
    kc = _head_norm(_compress_rows(Hk, w1tk[...], w1bk[...], pek[...], b1k[...], w2k[...]), gk[...]).astype(bf16)
    Hv = _page_half_blocks(page_refs, kvw, slab_ref, G).astype(bf16)
    vc = _compress_rows(Hv, w1tv[...], w1bv[...], pev[...], b1v[...], w2v[...]).astype(bf16)
    npg = Hk.shape[0] // G
    R = hg * n_tok
    scale = D ** -0.5
    t_row = past_len + lax.broadcasted_iota(jnp.int32, (R, 1), 0) % n_tok
    n_idx = lax.broadcasted_iota(jnp.int32, (1, npg), 1)
    vis = jnp.where(n_idx * CMP_STRIDE + (CMP_LEN - 1) <= t_row, 1.0, 0.0) * jnp.where(n_idx < nc, 1.0, 0.0)
    cur = (t_row // SEL_BLOCK).astype(f32)
    for g in range(G):
        q = q_ref[0, g]
        s = _dot_nt(q, kc[g * npg:(g + 1) * npg]) * scale
        p = _softmax_masked(s, vis)
        oc_ref[0, g] = jnp.dot(p.astype(bf16), vc[g * npg:(g + 1) * npg], preferred_element_type=f32)
        p_rep = _dot_split_left(rep_ref[...], p)
        imp = _dot_split(p_rep, c2s_ref[...])
        sel_ref[0, g] = _select_blocks(imp, cur, n_top)


def _dot_split_left(a, b):
    hi = b.astype(bf16)
    r1 = b - hi.astype(f32)
    mid = r1.astype(bf16)
    lo = (r1 - mid.astype(f32)).astype(bf16)
    return (jnp.dot(a, hi, preferred_element_type=f32) + jnp.dot(a, mid, preferred_element_type=f32)
            + jnp.dot(a, lo, preferred_element_type=f32))


def _page_specs(n_pages, page_rows, width, col_block):
    return [pl.BlockSpec((1, page_rows, width),
                         functools.partial(lambda b, pt, p, c: (pt[b, p], 0, c), p=p, c=col_block))
            for p in range(n_pages)]


def _sample_cmp(cache2, page_table, q_s, cwk, cwv, kc_norm_g, n_tok):
    B, n_pages = page_table.shape
    page_rows = cache2.shape[1]
    G, D = N_KV_HEADS, LANES
    hg = N_Q_HEADS // G
    R = hg * n_tok
    past_len = n_pages * page_rows
    L = past_len + n_tok
    nc = (L - CMP_LEN) // CMP_STRIDE + 1
    npg = past_len // CMP_STRIDE
    assert nc <= npg and npg % LANES == 0
    nsel = -(-L // SEL_BLOCK)
    nb = -(-nsel // LANES) * LANES
    c2s = _cmp_to_sel(npg, nb, nc, nsel)
    rep = jnp.asarray((np.arange(R)[:, None] % n_tok == np.arange(R)[None, :] % n_tok), dtype=bf16)
    gk = kc_norm_g.reshape(1, D)
    consts = list(cwk) + [gk] + list(cwv) + [c2s, rep]
    full = lambda a: pl.BlockSpec(a.shape, lambda b, pt: (0,) * a.ndim)
    est = 2 * n_pages * page_rows * 2 * G * D * 4 + 2 * sum(_nbytes(a.shape, a.dtype) for a in consts) \
        + 6 * G * npg * CMP_STRIDE * D * 4
    return pl.pallas_call(
        functools.partial(_sample_cmp_kernel, n_pages=n_pages, G=G, hg=hg, n_tok=n_tok,
                          n_top=min(SEL_TOP, nsel), past_len=past_len, nc=nc),
        grid_spec=pltpu.PrefetchScalarGridSpec(
            num_scalar_prefetch=1,
            grid=(B,),
            in_specs=_page_specs(n_pages, page_rows, 2 * G * D, 0)
            + [pl.BlockSpec((1, G, R, D), lambda b, pt: (b, 0, 0, 0))] + [full(a) for a in consts],
            out_specs=[pl.BlockSpec((1, G, R, D), lambda b, pt: (b, 0, 0, 0)),
                       pl.BlockSpec((1, G, R, nb), lambda b, pt: (b, 0, 0, 0))],
            scratch_shapes=[pltpu.VMEM((G * past_len, D), f32)]),
        out_shape=[jax.ShapeDtypeStruct((B, G, R, D), f32),
                   jax.ShapeDtypeStruct((B, G, R, nb), f32)],
        compiler_params=_cparams(("parallel",), est),
        name="sample_cmp",
    )(page_table, *([cache2] * n_pages), q_s, *consts)


def _sample_attn_kernel(pt_ref, *refs, n_pages, G, hg, n_tok, past_len):
    D = LANES
    page_refs = refs[:n_pages]
    q_ref, gate_ref, knew_ref, vnew_ref, kwnew_ref, vwnew_ref, win_ref, sel_ref, oc_ref, o_ref = refs[n_pages:]
    kvw = G * D
    R = hg * n_tok
    PR = page_refs[0].shape[1]
    n_win = win_ref.shape[1]
    scale = D ** -0.5
    t_row = past_len + lax.broadcasted_iota(jnp.int32, (R, 1), 0) % n_tok
    nb = sel_ref.shape[-1]
    row_blk = lax.broadcasted_iota(jnp.int32, (nb, PR), 0)
    col_blk = lax.broadcasted_iota(jnp.int32, (nb, PR), 1) // SEL_BLOCK
    new_pos = past_len + lax.broadcasted_iota(jnp.int32, (1, PR), 1)
    new_ok = jnp.where(new_pos <= t_row, 1.0, 0.0) * jnp.where(new_pos < past_len + n_tok, 1.0, 0.0)
    wp_old = past_len + n_tok - (n_win + n_tok) + lax.broadcasted_iota(jnp.int32, (1, n_win), 1)
    vis_old = jnp.where(wp_old <= t_row, 1.0, 0.0) * jnp.where(wp_old > t_row - WINDOW, 1.0, 0.0) \
        * jnp.where(wp_old >= 0, 1.0, 0.0)
    vis_wnew = new_ok * jnp.where(new_pos > t_row - WINDOW, 1.0, 0.0)
    zeros_new = jnp.zeros((PR - knew_ref.shape[2], D), bf16)
    gate = jax.nn.sigmoid(gate_ref[0])
    for g in range(G):
        q = q_ref[0, g]
        sel = sel_ref[0, g].astype(bf16)
        s_parts, m_parts = [], []
        for p in range(n_pages):
            kp = page_refs[p][0, :, g * D:(g + 1) * D].astype(bf16)
            s_parts.append(_dot_nt(q, kp))
            expand = jnp.where(row_blk - (p * PR) // SEL_BLOCK == col_blk, 1.0, 0.0).astype(bf16)
            m_parts.append(jnp.dot(sel, expand, preferred_element_type=f32))
        k_new = jnp.concatenate([knew_ref[0, g], zeros_new], axis=0)
        s_parts.append(_dot_nt(q, k_new))
        expand = jnp.where(row_blk - past_len // SEL_BLOCK == col_blk, 1.0, 0.0).astype(bf16)
        m_parts.append(jnp.dot(sel, expand, preferred_element_type=f32) * new_ok)
        s = jnp.concatenate(s_parts, axis=1) * scale
        p_s = _softmax_masked(s, jnp.concatenate(m_parts, axis=1)).astype(bf16)
        v_new = jnp.concatenate([vnew_ref[0, g], zeros_new], axis=0)
        o_s = jnp.dot(p_s[:, n_pages * PR:], v_new, preferred_element_type=f32)
        for p in range(n_pages):
            vp = page_refs[p][0, :, kvw + g * D:kvw + (g + 1) * D].astype(bf16)
            o_s = o_s + jnp.dot(p_s[:, p * PR:(p + 1) * PR], vp, preferred_element_type=f32)
        kw_old = win_ref[0, :, g * D:(g + 1) * D].astype(bf16)
        vw_old = win_ref[0, :, kvw + g * D:kvw + (g + 1) * D].astype(bf16)
        kw_new = jnp.concatenate([kwnew_ref[0, g], zeros_new], axis=0)
        vw_new = jnp.concatenate([vwnew_ref[0, g], zeros_new], axis=0)
        s = jnp.concatenate([_dot_nt(q, kw_old), _dot_nt(q, kw_new)], axis=1) * scale
        p_w = _softmax_masked(s, jnp.concatenate([vis_old, vis_wnew], axis=1)).astype(bf16)
        o_w = (jnp.dot(p_w[:, :n_win], vw_old, preferred_element_type=f32)
               + jnp.dot(p_w[:, n_win:], vw_new, preferred_element_type=f32))
        gg = gate[g]
        o_ref[0, g] = (oc_ref[0, g] * gg[:, 0:1] + o_s * gg[:, 1:2] + o_w * gg[:, 2:3]).astype(o_ref.dtype)


def _sample_attn(cache2, page_table, q_s, gate_s, new_rows, win_state, sel, o_c, n_tok):
    B, n_pages = page_table.shape
    page_rows = cache2.shape[1]
    G, D = N_KV_HEADS, LANES
    hg = N_Q_HEADS // G
    R = hg * n_tok
    past_len = n_pages * page_rows
    n_win = win_state.shape[1]
    nb = sel.shape[-1]
    P8 = new_rows[0].shape[2]
    per_b = lambda shp: pl.BlockSpec((1,) + shp, lambda b, pt: (b,) + (0,) * len(shp))
    est = 2 * n_pages * page_rows * 2 * G * D * 4 + 2 * n_win * 2 * G * D * 4 + 40 * R * (past_len + page_rows) * 4
    return pl.pallas_call(
        functools.partial(_sample_attn_kernel, n_pages=n_pages, G=G, hg=hg, n_tok=n_tok, past_len=past_len),
        grid_spec=pltpu.PrefetchScalarGridSpec(
            num_scalar_prefetch=1,
            grid=(B,),
            in_specs=_page_specs(n_pages, page_rows, 2 * G * D, 1)
            + [per_b((G, R, D)), per_b((G, R, LANES))] + [per_b((G, P8, D))] * 4
            + [per_b((n_win, 2 * G * D)), per_b((G, R, nb)), per_b((G, R, D))],
            out_specs=per_b((G, R, D))),
        out_shape=jax.ShapeDtypeStruct((B, G, R, D), bf16),
        compiler_params=_cparams(("parallel",), est),
        name="sample_attn",
    )(page_table, *([cache2] * n_pages), q_s, gate_s, *new_rows, win_state, sel, o_c)


def _pick_tile(n, pref):
    t = min(n, pref)
    while n % t:
        t //= 2
    return t


def _proj_layout(d_model, ret_w, qw, kvw):
    G = N_KV_HEADS
    lay, off = {}, 0
    for name, w in (("rq", ret_w), ("rk", ret_w), ("rv", ret_w), ("rg", ret_w), ("nq", qw), ("nkv", 6 * kvw),
                    ("gr", d_model), ("gn", d_model), ("gate", G * LANES)):
        lay[name] = off
        off += w
    lay["width"] = off
    lay["ret_dk"] = ret_w // N_RET_HEADS
    return lay


def _prep_w_in(w_in, d_model, ret_w, qw, kvw):
    G = N_KV_HEADS
    hg = N_Q_HEADS // G
    n_main = 4 * ret_w + qw + 6 * kvw
    ng = w_in[:, n_main:n_main + 3 * N_Q_HEADS].reshape(d_model, G, 3 * hg)
    ng = jnp.pad(ng, ((0, 0), (0, 0), (0, LANES - 3 * hg))).reshape(d_model, G * LANES)
    return jnp.concatenate([w_in[:, :n_main], w_in[:, n_main + 3 * N_Q_HEADS:], ng], axis=1).astype(bf16)


def _dense_tail(x, y_ret, y_nsa, proj, lay, wts, tm):
    w_ret_out, w_nsa_out, w_out, norm2_g, w_up, w_down = wts
    M, d_model = x.shape
    d_ff = w_up.shape[1]
    tn = _pick_tile(d_model, 512)
    m = _mm([(y_ret, 0, y_ret.shape[1], w_ret_out), (y_nsa, 0, y_nsa.shape[1], w_nsa_out)],
            [(proj, lay["gr"]), (proj, lay["gn"])], _ep_merge, bf16, tm, tn, name="merge")
    h = _mm([(m, 0, d_model, w_out)], [(x, 0)], _ep_residual, f32, tm, tn, name="out_proj")
    hn = _rmsnorm(h, norm2_g, _pick_tile(M, 256))
    u = _mm([(hn, 0, d_model, w_up)], [], _ep_relu2, bf16, tm, _pick_tile(d_ff, 512), name="mlp_up")
    return _mm([(u, 0, d_ff, w_down)], [(h, 0)], _ep_residual, f32, tm, _pick_tile(d_model, 1024),
               tk=_pick_tile(d_ff, 2048), name="mlp_down")


def kernel(x_prompt, x_sample, cache_kv, page_table, state_win_kv, state_ret, norm1_g, w_in, w_ret_out,
           q_norm_g, kc_norm_g, ks_norm_g, kw_norm_g, cmp_pe_k, cmp_w1_k, cmp_b1_k, cmp_w2_k,
           cmp_pe_v, cmp_w1_v, cmp_b1_v, cmp_w2_v, w_nsa_out, w_out, norm2_g, w_up, w_down):
    depth = w_in.shape[0]
    assert depth == 1 and x_prompt.shape[0] == 1
    B, T, d_model = x_prompt.shape
    Bs, Ts, _ = x_sample.shape
    G, D = N_KV_HEADS, LANES
    hg = N_Q_HEADS // G
    qw, kvw = N_Q_HEADS * D, G * D
    ret_w = w_ret_out.shape[1]
    dk = ret_w // N_RET_HEADS
    page_rows = cache_kv.shape[2]
    n_pages = page_table.shape[1]
    past_len = n_pages * page_rows
    lay = _proj_layout(d_model, ret_w, qw, kvw)

    l = 0
    w_in_b = _prep_w_in(w_in[l], d_model, ret_w, qw, kvw)
    wts = (w_ret_out[l].astype(bf16), w_nsa_out[l].astype(bf16), w_out[l].astype(bf16), norm2_g[l],
           w_up[l].astype(bf16), w_down[l].astype(bf16))
    cwk = _cmp_weights(cmp_pe_k[l], cmp_w1_k[l], cmp_b1_k[l], cmp_w2_k[l])
    cwv = _cmp_weights(cmp_pe_v[l], cmp_w1_v[l], cmp_b1_v[l], cmp_w2_v[l])
    tn_in = _pick_tile(lay["width"], 512)

    xp = x_prompt.reshape(T, d_model)
    tm = _pick_tile(T, 1024)
    xn = _rmsnorm(xp, norm1_g[l], _pick_tile(T, 256))
    proj = _mm([(xn, 0, d_model, w_in_b)], [], _ep_identity, f32, tm, tn_in, name="in_proj")
    qn, paged, win, kvb = _nsa_prep(proj, lay, q_norm_g[l], ks_norm_g[l], kw_norm_g[l], _pick_tile(T, 256))
    nhb = T // CMP_STRIDE
    half_blocks = (paged[:, :2 * kvw].astype(bf16).reshape(nhb, CMP_STRIDE, 2, G, D)
                   .transpose(2, 3, 0, 1, 4).reshape(2, G, nhb, CMP_STRIDE * D))
    kc = _compress_prompt(half_blocks[0], cwk, kc_norm_g[l], True)
    vc = _compress_prompt(half_blocks[1], cwv, kc_norm_g[l], False)
    y_nsa = _nsa_prompt(qn, proj, lay, kc, vc, kvb)
    y_ret, s_prompt = _ret_prompt(proj, lay, T)
    y_prompt = _dense_tail(xp, y_ret, y_nsa, proj, lay, wts, tm).reshape(B, T, d_model)
    new_kv_prompt = paged.reshape(1, B, T, 4, G, D)
    n_keep = min(WINDOW, T)
    new_win_prompt = win[T - n_keep:].reshape(1, B, n_keep, 2, G, D)
    new_ret_prompt = s_prompt.reshape(1, B, N_RET_HEADS, dk, dk)

    Ms = Bs * Ts
    xs = x_sample.reshape(Ms, d_model)
    tms = _pick_tile(Ms, 512)
    xn = _rmsnorm(xs, norm1_g[l], _pick_tile(Ms, 256))
    proj_s = _mm([(xn, 0, d_model, w_in_b)], [], _ep_identity, f32, tms, tn_in, name="in_proj_s")
    qn, paged_s, win_s, kvb = _nsa_prep(proj_s, lay, q_norm_g[l], ks_norm_g[l], kw_norm_g[l], _pick_tile(Ms, 256))
    P8 = 8
    assert Ts <= P8
    proj8 = jnp.pad(proj_s.reshape(Bs, Ts, lay["width"]), ((0, 0), (0, P8 - Ts), (0, 0)))
    y_ret8, s_sample = _ret_sample(proj8, lay, state_ret[l], past_len, Ts)
    y_ret = y_ret8[:, :Ts].reshape(Ms, ret_w)
    q_s = qn.reshape(Bs, Ts, G, hg, D).transpose(0, 2, 3, 1, 4).reshape(Bs, G, hg * Ts, D)
    gcol = lay["gate"]
    gate_s = (proj_s[:, gcol:gcol + G * LANES].reshape(Bs, Ts, G, LANES)[..., :3 * hg]
              .reshape(Bs, Ts, G, hg, 3).transpose(0, 2, 3, 1, 4).reshape(Bs, G, hg * Ts, 3))
    gate_s = jnp.pad(gate_s, ((0, 0), (0, 0), (0, 0), (0, LANES - 3)))
    kvb5 = jnp.pad(kvb.reshape(Bs, Ts, 4, G, D), ((0, 0), (0, P8 - Ts), (0, 0), (0, 0), (0, 0)))
    new_rows = [kvb5[:, :, c].transpose(0, 2, 1, 3) for c in range(4)]
    cache2 = cache_kv[l].reshape(cache_kv.shape[1], page_rows, 4 * kvw)
    win_state = state_win_kv[l].reshape(Bs, state_win_kv.shape[2], 2 * kvw)
    o_c, sel = _sample_cmp(cache2, page_table, q_s, cwk, cwv, kc_norm_g[l], Ts)
    o_s = _sample_attn(cache2, page_table, q_s, gate_s, new_rows, win_state, sel, o_c, Ts)
    y_nsa = o_s.reshape(Bs, G, hg, Ts, D).transpose(0, 3, 1, 2, 4).reshape(Ms, qw)
    y_sample = _dense_tail(xs, y_ret, y_nsa, proj_s, lay, wts, tms).reshape(Bs, Ts, d_model)
    new_kv_sample = paged_s.reshape(1, Bs, Ts, 4, G, D)
    n_state = state_win_kv.shape[2]
    win_all = jnp.concatenate([win_state, win_s.reshape(Bs, Ts, 2 * kvw)], axis=1)
    new_win_sample = win_all[:, win_all.shape[1] - n_state:].reshape(1, Bs, n_state, 2, G, D)
    new_ret_sample = s_sample.reshape(1, Bs, N_RET_HEADS, dk, dk)
    return (y_prompt, y_sample, new_kv_prompt, new_win_prompt, new_ret_prompt,
            new_kv_sample, new_win_sample, new_ret_sample)
```

```python
import functools
import math

import numpy as np
import jax
import jax.numpy as jnp
from jax import lax
from jax.experimental import pallas as pl
from jax.experimental.pallas import tpu as pltpu

N_RET_HEADS = 8
RET_CHUNK = 128
ROPE_BASE = 10000.0
N_Q_HEADS = 16
N_KV_HEADS = 4
CMP_LEN = 32
CMP_STRIDE = 16
SEL_BLOCK = 64
SEL_TOP = 16
WINDOW = 512
Q_BLOCK = 128
EPS = 1e-6

LANES = 128
VMEM_BYTES = 64 * 1024 * 1024
NEG = -1e30

f32 = jnp.float32
bf16 = jnp.bfloat16


def _cparams(sem, est_bytes):
    limit = int(min(VMEM_BYTES - (4 << 20), max(est_bytes + (8 << 20), 32 << 20)))
    return pltpu.CompilerParams(dimension_semantics=sem, vmem_limit_bytes=limit)


def _nbytes(shape, dtype):
    return int(np.prod(shape)) * jnp.dtype(dtype).itemsize


def _rmsnorm_kernel(x_ref, g_ref, o_ref):
    x = x_ref[...]
    y = x * lax.rsqrt(jnp.mean(x * x, axis=-1, keepdims=True) + EPS)
    o_ref[...] = (y * g_ref[...]).astype(o_ref.dtype)


def _rmsnorm(x, g, tm):
    M, D = x.shape
    return pl.pallas_call(
        _rmsnorm_kernel,
        grid=(M // tm,),
        in_specs=[pl.BlockSpec((tm, D), lambda i: (i, 0)),
                  pl.BlockSpec((1, D), lambda i: (0, 0))],
        out_specs=pl.BlockSpec((tm, D), lambda i: (i, 0)),
        out_shape=jax.ShapeDtypeStruct((M, D), bf16),
        compiler_params=_cparams(("parallel",), 2 * tm * D * 6),
        name="rmsnorm",
    )(x, g.reshape(1, D))


def _mm_kernel(*refs, n_pairs, n_extra, nk, epilogue):
    a = refs[:n_pairs]
    b = refs[n_pairs:2 * n_pairs]
    ex = refs[2 * n_pairs:2 * n_pairs + n_extra]
    o_ref = refs[2 * n_pairs + n_extra]
    if nk == 1:
        dots = [jnp.dot(a[i][...], b[i][...], preferred_element_type=f32) for i in range(n_pairs)]
        o_ref[...] = epilogue(dots, [e[...] for e in ex]).astype(o_ref.dtype)
    else:
        acc_ref = refs[-1]
        k = pl.program_id(2)

        @pl.when(k == 0)
        def _():
            acc_ref[...] = jnp.zeros_like(acc_ref)

        acc_ref[...] += jnp.dot(a[0][...], b[0][...], preferred_element_type=f32)

        @pl.when(k == nk - 1)
        def _():
            o_ref[...] = epilogue([acc_ref[...]], [e[...] for e in ex]).astype(o_ref.dtype)


def _mm(pairs, extras, epilogue, out_dtype, tm, tn, tk=None, name="mm"):
    M = pairs[0][0].shape[0]
    N = pairs[0][3].shape[1]
    K0 = pairs[0][2]
    tk = K0 if tk is None else tk
    nk = K0 // tk
    assert M % tm == 0 and N % tn == 0 and K0 % tk == 0
    assert nk == 1 or len(pairs) == 1
    in_specs, args, est = [], [], 0
    for (a, off, K, b) in pairs:
        kb = K if nk == 1 else tk
        assert off % kb == 0
        in_specs.append(pl.BlockSpec((tm, kb), functools.partial(lambda i, j, k, o: (i, o + k), o=off // kb)))
        args.append(a)
        est += 2 * _nbytes((tm, kb), a.dtype)
    for (a, off, K, b) in pairs:
        kb = K if nk == 1 else tk
        in_specs.append(pl.BlockSpec((kb, tn), lambda i, j, k: (k, j)))
        args.append(b)
        est += 2 * _nbytes((kb, tn), b.dtype)
    for (e, off) in extras:
        assert off % tn == 0
        in_specs.append(pl.BlockSpec((tm, tn), functools.partial(lambda i, j, k, o: (i, o + j), o=off // tn)))
        args.append(e)
        est += 2 * _nbytes((tm, tn), e.dtype)
    est += 2 * _nbytes((tm, tn), out_dtype) + (len(pairs) + 2) * tm * tn * 4
    scratch = [pltpu.VMEM((tm, tn), f32)] if nk > 1 else []
    return pl.pallas_call(
        functools.partial(_mm_kernel, n_pairs=len(pairs), n_extra=len(extras), nk=nk, epilogue=epilogue),
        grid=(M // tm, N // tn, nk),
        in_specs=in_specs,
        out_specs=pl.BlockSpec((tm, tn), lambda i, j, k: (i, j)),
        out_shape=jax.ShapeDtypeStruct((M, N), out_dtype),
        scratch_shapes=scratch,
        compiler_params=_cparams(("parallel", "parallel", "arbitrary"), est),
        name=name,
    )(*args)


def _ep_identity(dots, ex):
    return dots[0]


def _ep_merge(dots, ex):
    return jax.nn.sigmoid(ex[0]) * dots[0] + jax.nn.sigmoid(ex[1]) * dots[1]


def _ep_residual(dots, ex):
    return ex[0] + dots[0]


def _ep_relu2(dots, ex):
    return jnp.square(jnp.maximum(dots[0], 0.0))


def _head_norm(x, g):
    return x * lax.rsqrt(jnp.mean(x * x, axis=-1, keepdims=True) + EPS) * g


def _nsa_prep_kernel(q_ref, kv01_ref, kv23_ref, kv45_ref, qg_ref, ksg_ref, kwg_ref,
                     qn_ref, paged_ref, win_ref, kvb_ref, *, n_q, n_kv):
    D = LANES
    kvw = n_kv * D
    for h in range(n_q):
        sl = slice(h * D, (h + 1) * D)
        qn_ref[:, sl] = _head_norm(q_ref[:, sl], qg_ref[...]).astype(qn_ref.dtype)
    paged_ref[:, 0:2 * kvw] = kv01_ref[...]
    for g in range(n_kv):
        sl = slice(g * D, (g + 1) * D)
        ks = _head_norm(kv23_ref[:, sl], ksg_ref[...])
        paged_ref[:, 2 * kvw + g * D:2 * kvw + (g + 1) * D] = ks
        kvb_ref[:, sl] = ks.astype(kvb_ref.dtype)
        kw = _head_norm(kv45_ref[:, sl], kwg_ref[...])
        win_ref[:, sl] = kw
        kvb_ref[:, 2 * kvw + g * D:2 * kvw + (g + 1) * D] = kw.astype(kvb_ref.dtype)
    vs = kv23_ref[:, kvw:2 * kvw]
    paged_ref[:, 3 * kvw:4 * kvw] = vs
    kvb_ref[:, kvw:2 * kvw] = vs.astype(kvb_ref.dtype)
    vw = kv45_ref[:, kvw:2 * kvw]
    win_ref[:, kvw:2 * kvw] = vw
    kvb_ref[:, 3 * kvw:4 * kvw] = vw.astype(kvb_ref.dtype)


def _nsa_prep(proj, lay, q_norm_g, ks_norm_g, kw_norm_g, tm):
    M = proj.shape[0]
    n_q, n_kv, D = N_Q_HEADS, N_KV_HEADS, LANES
    qw, kvw = n_q * D, n_kv * D
    assert lay["nq"] % qw == 0 and lay["nkv"] % (2 * kvw) == 0
    kvb0 = lay["nkv"] // (2 * kvw)
    gspec = pl.BlockSpec((1, D), lambda i: (0, 0))
    return pl.pallas_call(
        functools.partial(_nsa_prep_kernel, n_q=n_q, n_kv=n_kv),
        grid=(M // tm,),
        in_specs=[pl.BlockSpec((tm, qw), functools.partial(lambda i, o: (i, o), o=lay["nq"] // qw)),
                  pl.BlockSpec((tm, 2 * kvw), functools.partial(lambda i, o: (i, o), o=kvb0)),
                  pl.BlockSpec((tm, 2 * kvw), functools.partial(lambda i, o: (i, o), o=kvb0 + 1)),
                  pl.BlockSpec((tm, 2 * kvw), functools.partial(lambda i, o: (i, o), o=kvb0 + 2)),
                  gspec, gspec, gspec],
        out_specs=[pl.BlockSpec((tm, qw), lambda i: (i, 0)),
                   pl.BlockSpec((tm, 4 * kvw), lambda i: (i, 0)),
                   pl.BlockSpec((tm, 2 * kvw), lambda i: (i, 0)),
                   pl.BlockSpec((tm, 4 * kvw), lambda i: (i, 0))],
        out_shape=[jax.ShapeDtypeStruct((M, qw), bf16),
                   jax.ShapeDtypeStruct((M, 4 * kvw), f32),
                   jax.ShapeDtypeStruct((M, 2 * kvw), f32),
                   jax.ShapeDtypeStruct((M, 4 * kvw), bf16)],
        compiler_params=_cparams(("parallel",), 2 * tm * (qw * 6 + kvw * 6 * 4 + kvw * 6 * 4 + kvw * 8)),
        name="nsa_prep",
    )(proj, proj, proj, proj, q_norm_g.reshape(1, D), ks_norm_g.reshape(1, D), kw_norm_g.reshape(1, D))


def _gelu_tanh(x):
    return 0.5 * x * (1.0 + jnp.tanh(math.sqrt(2.0 / math.pi) * (x + 0.044715 * (x * x * x))))


def _compress_rows(H, w1t, w1b, pe, b1, w2):
    half = w1t.shape[0]
    P = jnp.dot(H, w1t, preferred_element_type=f32)
    Q = jnp.dot(H, w1b, preferred_element_type=f32)
    bias = (jnp.dot(pe[:, :half], w1t, preferred_element_type=f32)
            + jnp.dot(pe[:, half:], w1b, preferred_element_type=f32))[0:1] + b1
    hidden = P + pltpu.roll(Q, Q.shape[0] - 1, 0) + bias
    return jnp.dot(_gelu_tanh(hidden).astype(bf16), w2, preferred_element_type=f32)


def _compress_kernel(h_ref, w1t_ref, w1b_ref, pe_ref, b1_ref, w2_ref, g_ref, o_ref, *, normalize):
    out = _compress_rows(h_ref[0], w1t_ref[...], w1b_ref[...], pe_ref[...], b1_ref[...], w2_ref[...])
    if normalize:
        out = _head_norm(out, g_ref[...])
    o_ref[0] = out.astype(o_ref.dtype)


def _cmp_weights(pe, w1, b1, w2):
    half = w1.shape[0] // 2
    pe8 = jnp.broadcast_to(pe.reshape(1, -1), (8, pe.size)).astype(bf16)
    return (w1[:half].astype(bf16), w1[half:].astype(bf16), pe8, b1.reshape(1, -1).astype(f32), w2.astype(bf16))


def _compress_prompt(H, cwk, g, normalize):
    G, n, W = H.shape
    w1t, w1b, pe8, b1, w2 = cwk
    hid, D = w2.shape
    full = lambda a: pl.BlockSpec(a.shape, lambda i: (0,) * a.ndim)
    g2 = g.reshape(1, D)
    return pl.pallas_call(
        functools.partial(_compress_kernel, normalize=normalize),
        grid=(G,),
        in_specs=[pl.BlockSpec((1, n, W), lambda i: (i, 0, 0)), full(w1t), full(w1b), full(pe8), full(b1), full(w2),
                  full(g2)],
        out_specs=pl.BlockSpec((1, n, D), lambda i: (i, 0, 0)),
        out_shape=jax.ShapeDtypeStruct((G, n, D), bf16),
        compiler_params=_cparams(("parallel",), 2 * (n * W * 2 + 2 * W * hid * 2) + 6 * n * hid * 4),
        name="compress_prompt",
    )(H, w1t, w1b, pe8, b1, w2, g2)


def _dot_nt(a, b):
    return lax.dot_general(a, b, (((1,), (1,)), ((), ())), preferred_element_type=f32)


def _dot_split(a, b):
    hi = a.astype(bf16)
    r1 = a - hi.astype(f32)
    mid = r1.astype(bf16)
    lo = (r1 - mid.astype(f32)).astype(bf16)
    return (jnp.dot(hi, b, preferred_element_type=f32) + jnp.dot(mid, b, preferred_element_type=f32)
            + jnp.dot(lo, b, preferred_element_type=f32))


def _softmax_masked(s, maskf):
    vis = maskf > 0.0
    sm = jnp.where(vis, s, NEG)
    m = jnp.max(sm, axis=-1, keepdims=True)
    e = jnp.where(vis, jnp.exp(sm - m), 0.0)
    return e / jnp.maximum(jnp.sum(e, axis=-1, keepdims=True), 1e-30)


def _select_blocks(imp, cur, n_top, axis):
    nb = imp.shape[axis]
    blk = lax.broadcasted_iota(jnp.int32, imp.shape, axis).astype(f32)
    forced = jnp.where(blk == 0.0, 1.0, 0.0) + jnp.where(blk == cur, 1.0, 0.0) + jnp.where(blk == cur - 1.0, 1.0, 0.0)
    score = jnp.where(forced > 0.0, jnp.inf, jnp.where(blk <= cur, imp, -jnp.inf))
    sel = jnp.zeros(imp.shape, f32)
    for _ in range(n_top):
        mx = jnp.max(score, axis=axis, keepdims=True)
        first = jnp.min(jnp.where(score == mx, blk, float(nb)), axis=axis, keepdims=True)
        hit = blk == first
        sel = jnp.where(hit, 1.0, sel)
        score = jnp.where(hit, -jnp.inf, score)
    return sel


def _nsa_prompt_kernel(q_ref, gate_ref, kc_ref, vc_ref, ks_ref, vs_ref, kw_ref, vw_ref, c2s_ref, kblk_ref, wbias_ref,
                       o_ref, *, hg, n_top, tk):
    D = LANES
    QB = q_ref.shape[0]
    T = ks_ref.shape[0]
    nc = kc_ref.shape[1]
    R = hg * QB
    c_exp = (D ** -0.5) * math.log2(math.e)
    qb = pl.program_id(1)
    s0 = qb * QB
    q = jnp.concatenate([q_ref[:, h * D:(h + 1) * D] for h in range(hg)], axis=0)
    t_row = s0 + (lax.broadcasted_iota(jnp.int32, (R, 1), 0) & (QB - 1))

    kc_end = lax.broadcasted_iota(jnp.int32, (1, nc), 1) * CMP_STRIDE + (CMP_LEN - 1)
    s = jnp.where(kc_end <= t_row, _dot_nt(q, kc_ref[0]), NEG)
    e = jnp.exp2((s - jnp.max(s, axis=-1, keepdims=True)) * c_exp)
    row_ok = jnp.where(t_row >= CMP_LEN - 1, 1.0, 0.0)
    inv = row_ok / jnp.maximum(jnp.sum(e, axis=-1, keepdims=True), 1e-30)
    o_c = jnp.dot(e.astype(bf16), vc_ref[0], preferred_element_type=f32) * inv

    p_sum = e[0:QB] * inv[0:QB]
    for h in range(1, hg):
        p_sum = p_sum + e[h * QB:(h + 1) * QB] * inv[h * QB:(h + 1) * QB]
    imp_t = _dot_split(p_sum, c2s_ref[...]).T
    cur_t = ((s0 + lax.broadcasted_iota(jnp.int32, (1, QB), 1)) // SEL_BLOCK).astype(f32)
    not_sel = (1.0 - _select_blocks(imp_t, cur_t, n_top, 0)).T.astype(bf16)

    q_aug = jnp.concatenate([q, jnp.concatenate([not_sel] * hg, axis=0)], axis=1)
    col_pos = lax.broadcasted_iota(jnp.int32, (1, tk), 1)

    def sel_tile(kt, carry, causal):
        m, l, acc = carry
        k0 = pl.multiple_of(kt * tk, tk)
        s = _dot_nt(q_aug, jnp.concatenate([ks_ref[pl.ds(k0, tk), :], kblk_ref[pl.ds(k0, tk), :]], axis=1))
        if causal:
            s = jnp.where(col_pos + k0 <= t_row, s, NEG)
        m_new = jnp.maximum(m, jnp.max(s, axis=-1, keepdims=True))
        alpha = jnp.exp2((m - m_new) * c_exp)
        e = jnp.exp2((s - m_new) * c_exp)
        l = alpha * l + jnp.sum(e, axis=-1, keepdims=True)
        acc = alpha * acc + jnp.dot(e.astype(bf16), vs_ref[pl.ds(k0, tk), :], preferred_element_type=f32)
        return m_new, l, acc

    last = (s0 + QB - 1) // tk
    init = (jnp.full((R, 1), NEG, f32), jnp.zeros((R, 1), f32), jnp.zeros((R, D), f32))
    carry = lax.fori_loop(0, last, functools.partial(sel_tile, causal=False), init)
    _, l, acc = sel_tile(last, carry, True)
    o_s = acc * (1.0 / jnp.maximum(l, 1e-30))

    wl = wbias_ref.shape[2]
    w0 = pl.multiple_of(jnp.clip(s0 - WINDOW, 0, T - wl), QB)
    bias = wbias_ref[jnp.minimum(qb, wbias_ref.shape[0] - 1)]
    s = _dot_nt(q, kw_ref[pl.ds(w0, wl), :])
    s = jnp.concatenate([s[h * QB:(h + 1) * QB] + bias for h in range(hg)], axis=0)
    e = jnp.exp2((s - jnp.max(s, axis=-1, keepdims=True)) * c_exp)
    o_w = (jnp.dot(e.astype(bf16), vw_ref[pl.ds(w0, wl), :], preferred_element_type=f32)
           * (1.0 / jnp.maximum(jnp.sum(e, axis=-1, keepdims=True), 1e-30)))

    gate = jax.nn.sigmoid(gate_ref[...])
    for h in range(hg):
        rows = slice(h * QB, (h + 1) * QB)
        o = (o_c[rows] * gate[:, 3 * h:3 * h + 1] + o_s[rows] * gate[:, 3 * h + 1:3 * h + 2]
             + o_w[rows] * gate[:, 3 * h + 2:3 * h + 3])
        o_ref[:, h * D:(h + 1) * D] = o.astype(o_ref.dtype)


def _cmp_to_sel(nc_pad, nsel_pad, nc, nsel):
    cs = np.arange(nc_pad)[:, None] * CMP_STRIDE
    bs = np.arange(nsel_pad)[None, :] * SEL_BLOCK
    ov = np.clip(np.minimum(cs + CMP_LEN, bs + SEL_BLOCK) - np.maximum(cs, bs), 0, None).astype(np.float32) / CMP_LEN
    ov[nc:, :] = 0.0
    ov[:, nsel:] = 0.0
    return jnp.asarray(ov, dtype=bf16)


def _key_block_onehot(n_keys, nb, first_block=0):
    blk = first_block + np.arange(n_keys)[:, None] // SEL_BLOCK
    return jnp.asarray(np.where(blk == np.arange(nb)[None, :], NEG, 0.0), dtype=bf16)


def _window_bias(QB, wl, n_off):
    d = (np.arange(n_off)[:, None, None] * QB + np.arange(QB)[None, :, None] - np.arange(wl)[None, None, :])
    return jnp.asarray(np.where((d >= 0) & (d < WINDOW), 0.0, NEG), dtype=f32)


def _nsa_prompt(qn, gates, kc, vc, kvb):
    T = qn.shape[0]
    G, D = N_KV_HEADS, LANES
    hg = N_Q_HEADS // G
    QB = Q_BLOCK
    nc_pad = kc.shape[1]
    nc = (T - CMP_LEN) // CMP_STRIDE + 1
    nsel = -(-T // SEL_BLOCK)
    nb = -(-nsel // LANES) * LANES
    c2s = _cmp_to_sel(nc_pad, nb, nc, nsel)
    kblk = _key_block_onehot(T, nb)
    tk = min(512, T)
    wl = WINDOW + QB
    assert T % tk == 0 and T % QB == 0 and WINDOW % QB == 0 and T >= wl and tk % QB == 0
    wbias = _window_bias(QB, wl, WINDOW // QB + 1)
    est = 2 * (4 * T * D * 2 + T * nb * 2 + wbias.size * 4) + 12 * hg * QB * max(tk, nc_pad, wl) * 4
    return pl.pallas_call(
        functools.partial(_nsa_prompt_kernel, hg=hg, n_top=min(SEL_TOP, nsel), tk=tk),
        grid=(G, T // QB),
        in_specs=[pl.BlockSpec((QB, hg * D), lambda g, i: (i, g)),
                  pl.BlockSpec((QB, LANES), lambda g, i: (i, g)),
                  pl.BlockSpec((1, nc_pad, D), lambda g, i: (g, 0, 0)),
                  pl.BlockSpec((1, nc_pad, D), lambda g, i: (g, 0, 0)),
                  pl.BlockSpec((T, D), lambda g, i: (0, g)),
                  pl.BlockSpec((T, D), functools.partial(lambda g, i, o: (0, o + g), o=G)),
                  pl.BlockSpec((T, D), functools.partial(lambda g, i, o: (0, o + g), o=2 * G)),
                  pl.BlockSpec((T, D), functools.partial(lambda g, i, o: (0, o + g), o=3 * G)),
                  pl.BlockSpec((nc_pad, nb), lambda g, i: (0, 0)),
                  pl.BlockSpec((T, nb), lambda g, i: (0, 0)),
                  pl.BlockSpec(wbias.shape, lambda g, i: (0, 0, 0))],
        out_specs=pl.BlockSpec((QB, hg * D), lambda g, i: (i, g)),
        out_shape=jax.ShapeDtypeStruct((T, N_Q_HEADS * D), bf16),
        compiler_params=_cparams(("parallel", "parallel"), est),
        name="nsa_prompt",
    )(qn, gates, kc, vc, kvb, kvb, kvb, kvb, c2s, kblk, wbias)


def _ret_tables(C, n_valid):
    lg = np.log1p(-np.exp2(-5.0 - np.arange(N_RET_HEADS, dtype=np.float64)))
    i = np.arange(C, dtype=np.float64)
    diff = i[:, None] - i[None, :]
    valid = (i < n_valid)
    dmask = np.where(diff >= 0, np.exp(lg[:, None, None] * np.maximum(diff, 0.0)), 0.0)
    dmask = dmask * (valid[:, None] & valid[None, :])
    q_dec = np.exp(lg[:, None] * (i[None, :] + 1.0)) * valid[None, :]
    k_dec = np.exp(lg[:, None] * (n_valid - 1.0 - i[None, :])) * valid[None, :]
    c_dec = np.exp(lg * n_valid)
    as32 = lambda a: jnp.asarray(a, dtype=f32)
    return (as32(dmask), as32(q_dec[:, :, None]), as32(k_dec[:, :, None]),
            as32(np.broadcast_to(c_dec[:, None, None], (N_RET_HEADS, 1, LANES))))


def _rope_tables(pos, half):
    inv = jnp.power(ROPE_BASE, -jnp.arange(half, dtype=f32) / half)
    ang = pos.astype(f32)[:, None] * inv[None, :]
    return jnp.cos(ang), jnp.sin(ang)


def _rotary(x, cos, sin):
    half = cos.shape[-1]
    x1, x2 = x[:, :half], x[:, half:]
    return jnp.concatenate([x1 * cos - x2 * sin, x1 * sin + x2 * cos], axis=-1)


def _ret_chunk(q_raw, k_raw, v, rg, cos, sin, dmask, q_dec, k_dec, c_dec, S):
    dk = q_raw.shape[-1]
    q = _rotary(q_raw, cos, sin)
    k = _rotary(k_raw, cos, sin) * (dk ** -0.5)
    vb = v.astype(bf16)
    inner = _dot_nt(q.astype(bf16), k.astype(bf16)) * dmask
    o = (jnp.dot(inner.astype(bf16), vb, preferred_element_type=f32)
         + jnp.dot((q * q_dec).astype(bf16), S.astype(bf16), preferred_element_type=f32))
    S_new = S * c_dec + lax.dot_general((k * k_dec).astype(bf16), vb, (((0,), (0,)), ((), ())),
                                        preferred_element_type=f32)
    y = o * lax.rsqrt(jnp.mean(o * o, axis=-1, keepdims=True) + EPS)
    return y * (rg * jax.nn.sigmoid(rg)), S_new


def _ret_prompt_kernel(q_ref, k_ref, v_ref, g_ref, cos_ref, sin_ref, dm_ref, qd_ref, kd_ref, cd_ref,
                       y_ref, s_out_ref, s_ref):
    c = pl.program_id(1)

    @pl.when(c == 0)
    def _():
        s_ref[...] = jnp.zeros_like(s_ref)

    y, S_new = _ret_chunk(q_ref[...], k_ref[...], v_ref[...], g_ref[...], cos_ref[...], sin_ref[...],
                          dm_ref[0], qd_ref[0], kd_ref[0], cd_ref[0][:, 0:1], s_ref[...])
    y_ref[...] = y.astype(y_ref.dtype)
    s_ref[...] = S_new

    @pl.when(c == pl.num_programs(1) - 1)
    def _():
        s_out_ref[0] = S_new


def _ret_prompt(proj, lay, T):
    H = N_RET_HEADS
    dk = lay["ret_dk"]
    C = min(RET_CHUNK, T)
    assert T % C == 0 and lay["rq"] % dk == 0
    cos, sin = _rope_tables(jnp.arange(T), dk // 2)
    dmask, q_dec, k_dec, c_dec = _ret_tables(C, C)
    col = lambda name: functools.partial(lambda h, c, o: (c, o + h), o=lay[name] // dk)
    tab = lambda shp: pl.BlockSpec((1,) + shp, lambda h, c: (h, 0, 0))
    return pl.pallas_call(
        _ret_prompt_kernel,
        grid=(H, T // C),
        in_specs=[pl.BlockSpec((C, dk), col("rq")), pl.BlockSpec((C, dk), col("rk")),
                  pl.BlockSpec((C, dk), col("rv")), pl.BlockSpec((C, dk), col("rg")),
                  pl.BlockSpec((C, dk // 2), lambda h, c: (c, 0)), pl.BlockSpec((C, dk // 2), lambda h, c: (c, 0)),
                  tab((C, C)), tab((C, 1)), tab((C, 1)), tab((1, LANES))],
        out_specs=[pl.BlockSpec((C, dk), lambda h, c: (c, h)),
                   pl.BlockSpec((1, dk, dk), lambda h, c: (h, 0, 0))],
        out_shape=[jax.ShapeDtypeStruct((T, H * dk), bf16),
                   jax.ShapeDtypeStruct((H, dk, dk), f32)],
        scratch_shapes=[pltpu.VMEM((dk, dk), f32)],
        compiler_params=_cparams(("parallel", "arbitrary"), 16 * C * dk * 4 + 4 * dk * dk * 4),
        name="ret_prompt",
    )(proj, proj, proj, proj, cos, sin, dmask, q_dec, k_dec, c_dec)


def _ret_sample_kernel(q_ref, k_ref, v_ref, g_ref, cos_ref, sin_ref, dm_ref, qd_ref, kd_ref, cd_ref, s0_ref,
                       y_ref, s_out_ref, *, n_heads, dk):
    for h in range(n_heads):
        sl = slice(h * dk, (h + 1) * dk)
        y, S_new = _ret_chunk(q_ref[0][:, sl], k_ref[0][:, sl], v_ref[0][:, sl], g_ref[0][:, sl],
                              cos_ref[...], sin_ref[...], dm_ref[h], qd_ref[h], kd_ref[h], cd_ref[h][:, 0:1],
                              s0_ref[0, h])
        y_ref[0, :, sl] = y.astype(y_ref.dtype)
        s_out_ref[0, h] = S_new


def _ret_sample(proj8, lay, state, past_len, n_tok):
    B, P, _ = proj8.shape
    H = N_RET_HEADS
    dk = lay["ret_dk"]
    w = H * dk
    assert all(lay[n] % w == 0 for n in ("rq", "rk", "rv", "rg"))
    cos, sin = _rope_tables(past_len + jnp.arange(P), dk // 2)
    dmask, q_dec, k_dec, c_dec = _ret_tables(P, n_tok)
    col = lambda name: functools.partial(lambda b, o: (b, 0, o), o=lay[name] // w)
    full = lambda a: pl.BlockSpec(a.shape, lambda b: (0,) * a.ndim)
    return pl.pallas_call(
        functools.partial(_ret_sample_kernel, n_heads=H, dk=dk),
        grid=(B,),
        in_specs=[pl.BlockSpec((1, P, w), col("rq")), pl.BlockSpec((1, P, w), col("rk")),
                  pl.BlockSpec((1, P, w), col("rv")), pl.BlockSpec((1, P, w), col("rg")),
                  full(cos), full(sin), full(dmask), full(q_dec), full(k_dec), full(c_dec),
                  pl.BlockSpec((1, H, dk, dk), lambda b: (b, 0, 0, 0))],
        out_specs=[pl.BlockSpec((1, P, w), lambda b: (b, 0, 0)),
                   pl.BlockSpec((1, H, dk, dk), lambda b: (b, 0, 0, 0))],
        out_shape=[jax.ShapeDtypeStruct((B, P, w), bf16),
                   jax.ShapeDtypeStruct((B, H, dk, dk), f32)],
        compiler_params=_cparams(("parallel",), 4 * H * dk * dk * 4 + 16 * P * w * 4),
        name="ret_sample",
    )(proj8, proj8, proj8, proj8, cos, sin, dmask, q_dec, k_dec, c_dec, state)


def _rows2d(r):
    return r.reshape(r.shape[0] * r.shape[1] * r.shape[2], r.shape[3])


def _slab(r2, kind, g, n_rows, pitch):
    return r2[pl.ds(kind * (pitch // 2) + g, n_rows, stride=pitch), :]


def _page_half_blocks(pages2, kind, G, page_rows):
    pitch = 2 * G
    n_hb = page_rows // CMP_STRIDE
    cols = []
    for j in range(CMP_STRIDE):
        cols.append(jnp.concatenate(
            [r2[pl.ds(j * pitch + kind * G + g, n_hb, stride=CMP_STRIDE * pitch), :]
             for g in range(G) for r2 in pages2], axis=0))
    return jnp.concatenate(cols, axis=1)


def _sample_cmp_kernel(pt_ref, *refs, n_pages, G, hg, n_tok, n_top, past_len, nc):
    D = LANES
    page_rows = refs[0].shape[0]
    pages2 = [_rows2d(r) for r in refs[:n_pages]]
    (q_ref, w1tk, w1bk, pek, b1k, w2k, gk, w1tv, w1bv, pev, b1v, w2v, c2s_ref, rep_ref,
     oc_ref, sel_ref) = refs[n_pages:]
    Hk = _page_half_blocks(pages2, 0, G, page_rows).astype(bf16)
    kc = _head_norm(_compress_rows(Hk, w1tk[...], w1bk[...], pek[...], b1k[...], w2k[...]), gk[...]).astype(bf16)
    Hv = _page_half_blocks(pages2, 1, G, page_rows).astype(bf16)
    vc = _compress_rows(Hv, w1tv[...], w1bv[...], pev[...], b1v[...], w2v[...]).astype(bf16)
    npg = Hk.shape[0] // G
    R = hg * n_tok
    RG = G * R
    nb = c2s_ref.shape[1]
    scale = D ** -0.5
    t_row = past_len + lax.broadcasted_iota(jnp.int32, (RG, 1), 0) % n_tok
    n_idx = lax.broadcasted_iota(jnp.int32, (1, npg), 1)
    vis = jnp.where(n_idx * CMP_STRIDE + (CMP_LEN - 1) <= t_row, 1.0, 0.0) * jnp.where(n_idx < nc, 1.0, 0.0)
    s = jnp.concatenate([_dot_nt(q_ref[0, g], kc[g * npg:(g + 1) * npg]) for g in range(G)], axis=0) * scale
    p = _softmax_masked(s, vis)
    for g in range(G):
        oc_ref[0, g] = jnp.dot(p[g * R:(g + 1) * R].astype(bf16), vc[g * npg:(g + 1) * npg],
                               preferred_element_type=f32)
    imp = _dot_split(_dot_split_left(rep_ref[...], p), c2s_ref[...])
    imp_t = jnp.concatenate([imp, jnp.zeros((LANES - RG, nb), f32)], axis=0).T
    cur_t = ((past_len + lax.broadcasted_iota(jnp.int32, (1, LANES), 1) % n_tok) // SEL_BLOCK).astype(f32)
    sel = _select_blocks(imp_t, cur_t, n_top, 0).T
    for g in range(G):
        sel_ref[0, g] = sel[g * R:(g + 1) * R]


def _dot_split_left(a, b):
    hi = b.astype(bf16)
    r1 = b - hi.astype(f32)
    mid = r1.astype(bf16)
    lo = (r1 - mid.astype(f32)).astype(bf16)
    return (jnp.dot(a, hi, preferred_element_type=f32) + jnp.dot(a, mid, preferred_element_type=f32)
            + jnp.dot(a, lo, preferred_element_type=f32))


def _page_specs(n_pages, page_rows, kind_pair, layer):
    return [pl.BlockSpec((None, None, page_rows, 2, N_KV_HEADS, LANES),
                         functools.partial(lambda b, pt, p: (layer, pt[b, p], 0, kind_pair, 0, 0), p=p))
            for p in range(n_pages)]


def _sample_cmp(cache_kv, layer, page_table, q_s, cwk, cwv, kc_norm_g, n_tok):
    B, n_pages = page_table.shape
    page_rows = cache_kv.shape[2]
    G, D = N_KV_HEADS, LANES
    hg = N_Q_HEADS // G
    R = hg * n_tok
    past_len = n_pages * page_rows
    L = past_len + n_tok
    nc = (L - CMP_LEN) // CMP_STRIDE + 1
    npg = past_len // CMP_STRIDE
    assert nc <= npg and npg % LANES == 0
    nsel = -(-L // SEL_BLOCK)
    nb = -(-nsel // LANES) * LANES
    c2s = _cmp_to_sel(npg, nb, nc, nsel)
    assert G * R <= LANES and R % n_tok == 0
    rr = np.arange(G * R)
    rep = jnp.asarray((rr[:, None] // R == rr[None, :] // R) & (rr[:, None] % n_tok == rr[None, :] % n_tok),
                      dtype=bf16)
    gk = kc_norm_g.reshape(1, D)
    consts = list(cwk) + [gk] + list(cwv) + [c2s, rep]
    full = lambda a: pl.BlockSpec(a.shape, lambda b, pt: (0,) * a.ndim)
    est = 2 * n_pages * page_rows * 2 * G * D * 4 + 2 * sum(_nbytes(a.shape, a.dtype) for a in consts) \
        + 6 * G * npg * CMP_STRIDE * D * 4
    return pl.pallas_call(
        functools.partial(_sample_cmp_kernel, n_pages=n_pages, G=G, hg=hg, n_tok=n_tok,
                          n_top=min(SEL_TOP, nsel), past_len=past_len, nc=nc),
        grid_spec=pltpu.PrefetchScalarGridSpec(
            num_scalar_prefetch=1,
            grid=(B,),
            in_specs=_page_specs(n_pages, page_rows, 0, layer)
            + [pl.BlockSpec((1, G, R, D), lambda b, pt: (b, 0, 0, 0))] + [full(a) for a in consts],
            out_specs=[pl.BlockSpec((1, G, R, D), lambda b, pt: (b, 0, 0, 0)),
                       pl.BlockSpec((1, G, R, nb), lambda b, pt: (b, 0, 0, 0))]),
        out_shape=[jax.ShapeDtypeStruct((B, G, R, D), f32),
                   jax.ShapeDtypeStruct((B, G, R, nb), f32)],
        compiler_params=_cparams(("parallel",), est),
        name="sample_cmp",
    )(page_table, *([cache_kv] * n_pages), q_s, *consts)


def _sample_attn_kernel(pt_ref, *refs, n_pages, G, hg, n_tok, past_len):
    D = LANES
    PR = refs[0].shape[0]
    pages2 = [_rows2d(r) for r in refs[:n_pages]]
    (q_ref, gate_ref, knew_ref, vnew_ref, kwnew_ref, vwnew_ref, wnew_ref, win_ref, sel_ref, oc_ref, kblk_ref,
     o_ref, nwin_ref) = refs[n_pages:]
    pitch = 2 * G
    R = hg * n_tok
    n_win = win_ref.shape[0]
    win2 = _rows2d(win_ref)
    c_exp = (D ** -0.5) * math.log2(math.e)
    t_row = past_len + lax.broadcasted_iota(jnp.int32, (R, 1), 0) % n_tok
    new_pos = past_len + lax.broadcasted_iota(jnp.int32, (1, PR), 1)
    new_ok = (new_pos <= t_row) & (new_pos < past_len + n_tok)
    wp_old = past_len - n_win + lax.broadcasted_iota(jnp.int32, (1, n_win), 1)
    vis_old = (wp_old <= t_row) & (wp_old > t_row - WINDOW) & (wp_old >= 0)
    vis_wnew = new_ok & (new_pos > t_row - WINDOW)
    zeros_new = jnp.zeros((PR - knew_ref.shape[2], D), bf16)
    gate = jax.nn.sigmoid(gate_ref[0])

    def softmax(s):
        e = jnp.exp2((s - jnp.max(s, axis=-1, keepdims=True)) * c_exp)
        return (e * (1.0 / jnp.maximum(jnp.sum(e, axis=-1, keepdims=True), 1e-30))).astype(bf16)

    for g in range(G):
        q = q_ref[0, g]
        q_aug = jnp.concatenate([q, (1.0 - sel_ref[0, g]).astype(bf16)], axis=1)
        s_parts = []
        for p in range(n_pages):
            kp = _slab(pages2[p], 0, g, PR, pitch).astype(bf16)
            s_parts.append(_dot_nt(q_aug, jnp.concatenate([kp, kblk_ref[p]], axis=1)))
        k_new = jnp.concatenate([knew_ref[0, g], zeros_new], axis=0)
        s_new = _dot_nt(q_aug, jnp.concatenate([k_new, kblk_ref[n_pages]], axis=1))
        s_parts.append(jnp.where(new_ok, s_new, NEG))
        p_s = softmax(jnp.concatenate(s_parts, axis=1))
        v_new = jnp.concatenate([vnew_ref[0, g], zeros_new], axis=0)
        o_s = jnp.dot(p_s[:, n_pages * PR:], v_new, preferred_element_type=f32)
        for p in range(n_pages):
            vp = _slab(pages2[p], 1, g, PR, pitch).astype(bf16)
            o_s = o_s + jnp.dot(p_s[:, p * PR:(p + 1) * PR], vp, preferred_element_type=f32)
        kw_old = _slab(win2, 0, g, n_win, pitch).astype(bf16)
        vw_old = _slab(win2, 1, g, n_win, pitch).astype(bf16)
        kw_new = jnp.concatenate([kwnew_ref[0, g], zeros_new], axis=0)
        vw_new = jnp.concatenate([vwnew_ref[0, g], zeros_new], axis=0)
        p_w = softmax(jnp.concatenate([jnp.where(vis_old, _dot_nt(q, kw_old), NEG),
                                       jnp.where(vis_wnew, _dot_nt(q, kw_new), NEG)], axis=1))
        o_w = (jnp.dot(p_w[:, :n_win], vw_old, preferred_element_type=f32)
               + jnp.dot(p_w[:, n_win:], vw_new, preferred_element_type=f32))
        gg = gate[g]
        o_ref[0, g] = (oc_ref[0, g] * gg[:, 0:1] + o_s * gg[:, 1:2] + o_w * gg[:, 2:3]).astype(o_ref.dtype)
    nwin_ref[0:n_win - n_tok] = win_ref[n_tok:n_win]
    nwin_ref[n_win - n_tok:n_win] = wnew_ref[...]


def _sample_attn(cache_kv, layer, page_table, q_s, gate_s, new_rows, win_new, state_win_kv, sel, o_c, n_tok):
    B, n_pages = page_table.shape
    page_rows = cache_kv.shape[2]
    G, D = N_KV_HEADS, LANES
    hg = N_Q_HEADS // G
    R = hg * n_tok
    past_len = n_pages * page_rows
    n_win = state_win_kv.shape[2]
    nb = sel.shape[-1]
    P8 = new_rows[0].shape[2]
    assert page_rows % SEL_BLOCK == 0 and nb == LANES and n_win >= n_tok
    kblk = jnp.stack([_key_block_onehot(page_rows, nb, (p * page_rows) // SEL_BLOCK) for p in range(n_pages + 1)])
    per_b = lambda shp: pl.BlockSpec((1,) + shp, lambda b, pt: (b,) + (0,) * len(shp))
    win_spec = pl.BlockSpec((None, None, n_win, 2, G, D), lambda b, pt: (layer, b, 0, 0, 0, 0))
    est = 2 * n_pages * page_rows * 2 * G * D * 4 + 4 * n_win * 2 * G * D * 4 + 40 * R * (past_len + page_rows) * 4
    return pl.pallas_call(
        functools.partial(_sample_attn_kernel, n_pages=n_pages, G=G, hg=hg, n_tok=n_tok, past_len=past_len),
        grid_spec=pltpu.PrefetchScalarGridSpec(
            num_scalar_prefetch=1,
            grid=(B,),
            in_specs=_page_specs(n_pages, page_rows, 1, layer)
            + [per_b((G, R, D)), per_b((G, R, LANES))] + [per_b((G, P8, D))] * 4
            + [pl.BlockSpec((None, n_tok, 2, G, D), lambda b, pt: (b, 0, 0, 0, 0)), win_spec,
               per_b((G, R, nb)), per_b((G, R, D)), pl.BlockSpec(kblk.shape, lambda b, pt: (0, 0, 0))],
            out_specs=[per_b((G, R, D)), win_spec]),
        out_shape=[jax.ShapeDtypeStruct((B, G, R, D), bf16),
                   jax.ShapeDtypeStruct(state_win_kv.shape, state_win_kv.dtype)],
        compiler_params=_cparams(("parallel",), est),
        name="sample_attn",
    )(page_table, *([cache_kv] * n_pages), q_s, gate_s, *new_rows, win_new, state_win_kv, sel, o_c, kblk)


def _pick_tile(n, pref):
    t = min(n, pref)
    while n % t:
        t //= 2
    return t


def _proj_layout(ret_w, qw, kvw):
    lay, off = {}, 0
    for name, w in (("rq", ret_w), ("rk", ret_w), ("rv", ret_w), ("rg", ret_w), ("nq", qw), ("nkv", 6 * kvw)):
        lay[name] = off
        off += w
    lay["width"] = off
    lay["ret_dk"] = ret_w // N_RET_HEADS
    return lay


def _prep_w_in(w_in, d_model, n_main):
    G = N_KV_HEADS
    hg = N_Q_HEADS // G
    ng = w_in[:, n_main:n_main + 3 * N_Q_HEADS].reshape(d_model, G, 3 * hg)
    ng = jnp.pad(ng, ((0, 0), (0, 0), (0, LANES - 3 * hg))).reshape(d_model, G * LANES)
    return (w_in[:, :n_main].astype(bf16), w_in[:, n_main + 3 * N_Q_HEADS:].astype(bf16), ng.astype(bf16))


def _in_proj(xn, w_in_parts, tm, tag):
    return [_mm([(xn, 0, xn.shape[1], w)], [], _ep_identity, f32, tm, _pick_tile(w.shape[1], 512),
                name="in_proj_%s%d" % (tag, i)) for i, w in enumerate(w_in_parts)]


def _dense_tail(x, y_ret, y_nsa, proj_g, wts, tm):
    w_ret_out, w_nsa_out, w_out, norm2_g, w_up, w_down = wts
    M, d_model = x.shape
    d_ff = w_up.shape[1]
    tn = _pick_tile(d_model, 512)
    m = _mm([(y_ret, 0, y_ret.shape[1], w_ret_out), (y_nsa, 0, y_nsa.shape[1], w_nsa_out)],
            [(proj_g, 0), (proj_g, d_model)], _ep_merge, bf16, tm, tn, name="merge")
    h = _mm([(m, 0, d_model, w_out)], [(x, 0)], _ep_residual, f32, tm, tn, name="out_proj")
    hn = _rmsnorm(h, norm2_g, _pick_tile(M, 256))
    u = _mm([(hn, 0, d_model, w_up)], [], _ep_relu2, bf16, tm, _pick_tile(d_ff, 512), name="mlp_up")
    return _mm([(u, 0, d_ff, w_down)], [(h, 0)], _ep_residual, f32, tm, _pick_tile(d_model, 1024),
               tk=_pick_tile(d_ff, 2048), name="mlp_down")


def kernel(x_prompt, x_sample, cache_kv, page_table, state_win_kv, state_ret, norm1_g, w_in, w_ret_out,
           q_norm_g, kc_norm_g, ks_norm_g, kw_norm_g, cmp_pe_k, cmp_w1_k, cmp_b1_k, cmp_w2_k,
           cmp_pe_v, cmp_w1_v, cmp_b1_v, cmp_w2_v, w_nsa_out, w_out, norm2_g, w_up, w_down):
    depth = w_in.shape[0]
    assert depth == 1 and x_prompt.shape[0] == 1
    B, T, d_model = x_prompt.shape
    Bs, Ts, _ = x_sample.shape
    G, D = N_KV_HEADS, LANES
    hg = N_Q_HEADS // G
    qw, kvw = N_Q_HEADS * D, G * D
    ret_w = w_ret_out.shape[1]
    dk = ret_w // N_RET_HEADS
    page_rows = cache_kv.shape[2]
    n_pages = page_table.shape[1]
    past_len = n_pages * page_rows
    lay = _proj_layout(ret_w, qw, kvw)

    l = 0
    w_in_parts = _prep_w_in(w_in[l], d_model, lay["width"])
    wts = (w_ret_out[l].astype(bf16), w_nsa_out[l].astype(bf16), w_out[l].astype(bf16), norm2_g[l],
           w_up[l].astype(bf16), w_down[l].astype(bf16))
    cwk = _cmp_weights(cmp_pe_k[l], cmp_w1_k[l], cmp_b1_k[l], cmp_w2_k[l])
    cwv = _cmp_weights(cmp_pe_v[l], cmp_w1_v[l], cmp_b1_v[l], cmp_w2_v[l])

    xp = x_prompt.reshape(T, d_model)
    tm = _pick_tile(T, 1024)
    xn = _rmsnorm(xp, norm1_g[l], _pick_tile(T, 256))
    proj, proj_g, gates = _in_proj(xn, w_in_parts, tm, "p")
    qn, paged, win, kvb = _nsa_prep(proj, lay, q_norm_g[l], ks_norm_g[l], kw_norm_g[l], _pick_tile(T, 256))
    nhb = T // CMP_STRIDE
    half_blocks = (paged[:, :2 * kvw].astype(bf16).reshape(nhb, CMP_STRIDE, 2, G, D)
                   .transpose(2, 3, 0, 1, 4).reshape(2, G, nhb, CMP_STRIDE * D))
    kc = _compress_prompt(half_blocks[0], cwk, kc_norm_g[l], True)
    vc = _compress_prompt(half_blocks[1], cwv, kc_norm_g[l], False)
    y_nsa = _nsa_prompt(qn, gates, kc, vc, kvb)
    y_ret, s_prompt = _ret_prompt(proj, lay, T)
    y_prompt = _dense_tail(xp, y_ret, y_nsa, proj_g, wts, tm).reshape(B, T, d_model)
    new_kv_prompt = paged.reshape(1, B, T, 4, G, D)
    n_keep = min(WINDOW, T)
    new_win_prompt = win[T - n_keep:].reshape(1, B, n_keep, 2, G, D)
    new_ret_prompt = s_prompt.reshape(1, B, N_RET_HEADS, dk, dk)

    Ms = Bs * Ts
    xs = x_sample.reshape(Ms, d_model)
    tms = _pick_tile(Ms, 512)
    xn = _rmsnorm(xs, norm1_g[l], _pick_tile(Ms, 256))
    proj_s, proj_gs, gates = _in_proj(xn, w_in_parts, tms, "s")
    qn, paged_s, win_s, kvb = _nsa_prep(proj_s, lay, q_norm_g[l], ks_norm_g[l], kw_norm_g[l], _pick_tile(Ms, 256))
    P8 = 8
    assert Ts <= P8
    proj8 = jnp.pad(proj_s.reshape(Bs, Ts, lay["width"]), ((0, 0), (0, P8 - Ts), (0, 0)))
    y_ret8, s_sample = _ret_sample(proj8, lay, state_ret[l], past_len, Ts)
    y_ret = y_ret8[:, :Ts].reshape(Ms, ret_w)
    q_s = qn.reshape(Bs, Ts, G, hg, D).transpose(0, 2, 3, 1, 4).reshape(Bs, G, hg * Ts, D)
    gate_s = (gates.reshape(Bs, Ts, G, LANES)[..., :3 * hg]
              .reshape(Bs, Ts, G, hg, 3).transpose(0, 2, 3, 1, 4).reshape(Bs, G, hg * Ts, 3))
    gate_s = jnp.pad(gate_s, ((0, 0), (0, 0), (0, 0), (0, LANES - 3)))
    kvb5 = jnp.pad(kvb.reshape(Bs, Ts, 4, G, D), ((0, 0), (0, P8 - Ts), (0, 0), (0, 0), (0, 0)))
    new_rows = [kvb5[:, :, c].transpose(0, 2, 1, 3) for c in range(4)]
    win_new = win_s.reshape(Bs, Ts, 2, G, D)
    o_c, sel = _sample_cmp(cache_kv, l, page_table, q_s, cwk, cwv, kc_norm_g[l], Ts)
    o_s, new_win_sample = _sample_attn(cache_kv, l, page_table, q_s, gate_s, new_rows, win_new, state_win_kv,
                                       sel, o_c, Ts)
    y_nsa = o_s.reshape(Bs, G, hg, Ts, D).transpose(0, 3, 1, 2, 4).reshape(Ms, qw)
    y_sample = _dense_tail(xs, y_ret, y_nsa, proj_gs, wts, tms).reshape(Bs, Ts, d_model)
    new_kv_sample = paged_s.reshape(1, Bs, Ts, 4, G, D)
    new_ret_sample = s_sample.reshape(1, Bs, N_RET_HEADS, dk, dk)
    return (y_prompt, y_sample, new_kv_prompt, new_win_prompt, new_ret_prompt,
            new_kv_sample, new_win_sample, new_ret_sample)
```

```python
import functools
import math

import numpy as np
import jax
import jax.numpy as jnp
from jax import lax
from jax.experimental import pallas as pl
from jax.experimental.pallas import tpu as pltpu

N_RET_HEADS = 8
RET_CHUNK = 128
ROPE_BASE = 10000.0
N_Q_HEADS = 16
N_KV_HEADS = 4
CMP_LEN = 32
CMP_STRIDE = 16
SEL_BLOCK = 64
SEL_TOP = 16
WINDOW = 512
Q_BLOCK = 128
EPS = 1e-6

LANES = 128
VMEM_BYTES = 64 * 1024 * 1024
NEG = -1e30
SEL_TILE = 1024

f32 = jnp.float32
bf16 = jnp.bfloat16


def _cparams(sem, est_bytes):
    limit = int(min(VMEM_BYTES - (4 << 20), max(est_bytes + (8 << 20), 32 << 20)))
    return pltpu.CompilerParams(dimension_semantics=sem, vmem_limit_bytes=limit)


def _nbytes(shape, dtype):
    return int(np.prod(shape)) * jnp.dtype(dtype).itemsize


def _rmsnorm_kernel(x_ref, g_ref, o_ref):
    x = x_ref[...]
    y = x * lax.rsqrt(jnp.mean(x * x, axis=-1, keepdims=True) + EPS)
    o_ref[...] = (y * g_ref[...]).astype(o_ref.dtype)


def _rmsnorm(x, g, tm):
    M, D = x.shape
    return pl.pallas_call(
        _rmsnorm_kernel,
        grid=(M // tm,),
        in_specs=[pl.BlockSpec((tm, D), lambda i: (i, 0)),
                  pl.BlockSpec((1, D), lambda i: (0, 0))],
        out_specs=pl.BlockSpec((tm, D), lambda i: (i, 0)),
        out_shape=jax.ShapeDtypeStruct((M, D), bf16),
        compiler_params=_cparams(("parallel",), 2 * tm * D * 6),
        name="rmsnorm",
    )(x, g.reshape(1, D))


def _mm_kernel(*refs, n_pairs, n_extra, nk, epilogue):
    a = refs[:n_pairs]
    b = refs[n_pairs:2 * n_pairs]
    ex = refs[2 * n_pairs:2 * n_pairs + n_extra]
    o_ref = refs[2 * n_pairs + n_extra]
    if nk == 1:
        dots = [jnp.dot(a[i][...], b[i][...].astype(bf16), preferred_element_type=f32) for i in range(n_pairs)]
        o_ref[...] = epilogue(dots, [e[...] for e in ex]).astype(o_ref.dtype)
    else:
        acc_ref = refs[-1]
        k = pl.program_id(2)

        @pl.when(k == 0)
        def _():
            acc_ref[...] = jnp.zeros_like(acc_ref)

        acc_ref[...] += jnp.dot(a[0][...], b[0][...].astype(bf16), preferred_element_type=f32)

        @pl.when(k == nk - 1)
        def _():
            o_ref[...] = epilogue([acc_ref[...]], [e[...] for e in ex]).astype(o_ref.dtype)


def _mm(pairs, extras, epilogue, out_dtype, tm, tn, tk=None, n_out=None, name="mm"):
    M = pairs[0][0].shape[0]
    N = pairs[0][3].shape[1] if n_out is None else n_out
    K0 = pairs[0][2]
    tk = K0 if tk is None else tk
    nk = K0 // tk
    assert M % tm == 0 and N % tn == 0 and K0 % tk == 0
    assert nk == 1 or len(pairs) == 1
    in_specs, args, est = [], [], 0
    for (a, off, K, b) in pairs:
        kb = K if nk == 1 else tk
        assert off % kb == 0
        in_specs.append(pl.BlockSpec((tm, kb), functools.partial(lambda i, j, k, o: (i, o + k), o=off // kb)))
        args.append(a)
        est += 2 * _nbytes((tm, kb), a.dtype)
    for (a, off, K, b) in pairs:
        kb = K if nk == 1 else tk
        in_specs.append(pl.BlockSpec((kb, tn), lambda i, j, k: (k, j)))
        args.append(b)
        est += 2 * _nbytes((kb, tn), b.dtype) + (0 if b.dtype == bf16 else _nbytes((kb, tn), bf16))
    for (e, off) in extras:
        assert off % tn == 0
        in_specs.append(pl.BlockSpec((tm, tn), functools.partial(lambda i, j, k, o: (i, o + j), o=off // tn)))
        args.append(e)
        est += 2 * _nbytes((tm, tn), e.dtype)
    est += 2 * _nbytes((tm, tn), out_dtype) + (len(pairs) + 2) * tm * tn * 4
    scratch = [pltpu.VMEM((tm, tn), f32)] if nk > 1 else []
    return pl.pallas_call(
        functools.partial(_mm_kernel, n_pairs=len(pairs), n_extra=len(extras), nk=nk, epilogue=epilogue),
        grid=(M // tm, N // tn, nk),
        in_specs=in_specs,
        out_specs=pl.BlockSpec((tm, tn), lambda i, j, k: (i, j)),
        out_shape=jax.ShapeDtypeStruct((M, N), out_dtype),
        scratch_shapes=scratch,
        compiler_params=_cparams(("parallel", "parallel", "arbitrary"), est),
        name=name,
    )(*args)


def _ep_identity(dots, ex):
    return dots[0]


def _ep_merge(dots, ex):
    return jax.nn.sigmoid(ex[0]) * dots[0] + jax.nn.sigmoid(ex[1]) * dots[1]


def _ep_residual(dots, ex):
    return ex[0] + dots[0]


def _ep_relu2(dots, ex):
    return jnp.square(jnp.maximum(dots[0], 0.0))


def _head_norm(x, g):
    return x * lax.rsqrt(jnp.mean(x * x, axis=-1, keepdims=True) + EPS) * g


def _nsa_prep_kernel(q_ref, kv01_ref, kv23_ref, kv45_ref, qg_ref, ksg_ref, kwg_ref,
                     qn_ref, paged_ref, win_ref, kvb_ref, *, n_q, n_kv):
    D = LANES
    kvw = n_kv * D
    for h in range(n_q):
        sl = slice(h * D, (h + 1) * D)
        qn_ref[:, sl] = _head_norm(q_ref[:, sl], qg_ref[...]).astype(qn_ref.dtype)
    paged_ref[:, 0:2 * kvw] = kv01_ref[...]
    for g in range(n_kv):
        sl = slice(g * D, (g + 1) * D)
        ks = _head_norm(kv23_ref[:, sl], ksg_ref[...])
        paged_ref[:, 2 * kvw + g * D:2 * kvw + (g + 1) * D] = ks
        kvb_ref[:, sl] = ks.astype(kvb_ref.dtype)
        kw = _head_norm(kv45_ref[:, sl], kwg_ref[...])
        win_ref[:, sl] = kw
        kvb_ref[:, 2 * kvw + g * D:2 * kvw + (g + 1) * D] = kw.astype(kvb_ref.dtype)
    vs = kv23_ref[:, kvw:2 * kvw]
    paged_ref[:, 3 * kvw:4 * kvw] = vs
    kvb_ref[:, kvw:2 * kvw] = vs.astype(kvb_ref.dtype)
    vw = kv45_ref[:, kvw:2 * kvw]
    win_ref[:, kvw:2 * kvw] = vw
    kvb_ref[:, 3 * kvw:4 * kvw] = vw.astype(kvb_ref.dtype)


def _nsa_prep(proj, lay, q_norm_g, ks_norm_g, kw_norm_g, tm):
    M = proj.shape[0]
    n_q, n_kv, D = N_Q_HEADS, N_KV_HEADS, LANES
    qw, kvw = n_q * D, n_kv * D
    assert lay["nq"] % qw == 0 and lay["nkv"] % (2 * kvw) == 0
    kvb0 = lay["nkv"] // (2 * kvw)
    gspec = pl.BlockSpec((1, D), lambda i: (0, 0))
    return pl.pallas_call(
        functools.partial(_nsa_prep_kernel, n_q=n_q, n_kv=n_kv),
        grid=(M // tm,),
        in_specs=[pl.BlockSpec((tm, qw), functools.partial(lambda i, o: (i, o), o=lay["nq"] // qw)),
                  pl.BlockSpec((tm, 2 * kvw), functools.partial(lambda i, o: (i, o), o=kvb0)),
                  pl.BlockSpec((tm, 2 * kvw), functools.partial(lambda i, o: (i, o), o=kvb0 + 1)),
                  pl.BlockSpec((tm, 2 * kvw), functools.partial(lambda i, o: (i, o), o=kvb0 + 2)),
                  gspec, gspec, gspec],
        out_specs=[pl.BlockSpec((tm, qw), lambda i: (i, 0)),
                   pl.BlockSpec((tm, 4 * kvw), lambda i: (i, 0)),
                   pl.BlockSpec((tm, 2 * kvw), lambda i: (i, 0)),
                   pl.BlockSpec((tm, 4 * kvw), lambda i: (i, 0))],
        out_shape=[jax.ShapeDtypeStruct((M, qw), bf16),
                   jax.ShapeDtypeStruct((M, 4 * kvw), f32),
                   jax.ShapeDtypeStruct((M, 2 * kvw), f32),
                   jax.ShapeDtypeStruct((M, 4 * kvw), bf16)],
        compiler_params=_cparams(("parallel",), 2 * tm * (qw * 6 + kvw * 6 * 4 + kvw * 6 * 4 + kvw * 8)),
        name="nsa_prep",
    )(proj, proj, proj, proj, q_norm_g.reshape(1, D), ks_norm_g.reshape(1, D), kw_norm_g.reshape(1, D))


def _gelu_tanh(x):
    return 0.5 * x * (1.0 + jnp.tanh(math.sqrt(2.0 / math.pi) * (x + 0.044715 * (x * x * x))))


def _compress_rows(H, w1t, w1b, pe, b1, w2):
    half = w1t.shape[0]
    P = jnp.dot(H, w1t, preferred_element_type=f32)
    Q = jnp.dot(H, w1b, preferred_element_type=f32)
    bias = (jnp.dot(pe[:, :half], w1t, preferred_element_type=f32)
            + jnp.dot(pe[:, half:], w1b, preferred_element_type=f32))[0:1] + b1
    hidden = P + pltpu.roll(Q, Q.shape[0] - 1, 0) + bias
    return jnp.dot(_gelu_tanh(hidden).astype(bf16), w2, preferred_element_type=f32)


def _compress_kernel(h_ref, w1t_ref, w1b_ref, pe_ref, b1_ref, w2_ref, g_ref, o_ref, *, normalize):
    out = _compress_rows(h_ref[0], w1t_ref[...], w1b_ref[...], pe_ref[...], b1_ref[...], w2_ref[...])
    if normalize:
        out = _head_norm(out, g_ref[...])
    o_ref[0] = out.astype(o_ref.dtype)


def _cmp_weights(pe, w1, b1, w2):
    half = w1.shape[0] // 2
    pe8 = jnp.broadcast_to(pe.reshape(1, -1), (8, pe.size)).astype(bf16)
    return (w1[:half].astype(bf16), w1[half:].astype(bf16), pe8, b1.reshape(1, -1).astype(f32), w2.astype(bf16))


def _compress_prompt(H, cwk, g, normalize):
    G, n, W = H.shape
    w1t, w1b, pe8, b1, w2 = cwk
    hid, D = w2.shape
    full = lambda a: pl.BlockSpec(a.shape, lambda i: (0,) * a.ndim)
    g2 = g.reshape(1, D)
    return pl.pallas_call(
        functools.partial(_compress_kernel, normalize=normalize),
        grid=(G,),
        in_specs=[pl.BlockSpec((1, n, W), lambda i: (i, 0, 0)), full(w1t), full(w1b), full(pe8), full(b1), full(w2),
                  full(g2)],
        out_specs=pl.BlockSpec((1, n, D), lambda i: (i, 0, 0)),
        out_shape=jax.ShapeDtypeStruct((G, n, D), bf16),
        compiler_params=_cparams(("parallel",), 2 * (n * W * 2 + 2 * W * hid * 2) + 6 * n * hid * 4),
        name="compress_prompt",
    )(H, w1t, w1b, pe8, b1, w2, g2)


def _dot_nt(a, b):
    return lax.dot_general(a, b, (((1,), (1,)), ((), ())), preferred_element_type=f32)


def _dot_split(a, b):
    hi = a.astype(bf16)
    r1 = a - hi.astype(f32)
    mid = r1.astype(bf16)
    lo = (r1 - mid.astype(f32)).astype(bf16)
    return (jnp.dot(hi, b, preferred_element_type=f32) + jnp.dot(mid, b, preferred_element_type=f32)
            + jnp.dot(lo, b, preferred_element_type=f32))


def _softmax_masked(s, maskf):
    vis = maskf > 0.0
    sm = jnp.where(vis, s, NEG)
    m = jnp.max(sm, axis=-1, keepdims=True)
    e = jnp.where(vis, jnp.exp(sm - m), 0.0)
    return e / jnp.maximum(jnp.sum(e, axis=-1, keepdims=True), 1e-30)


def _select_blocks(imp, cur, n_top, axis):
    nb = imp.shape[axis]
    blk = lax.broadcasted_iota(jnp.int32, imp.shape, axis).astype(f32)
    forced = jnp.where(blk == 0.0, 1.0, 0.0) + jnp.where(blk == cur, 1.0, 0.0) + jnp.where(blk == cur - 1.0, 1.0, 0.0)
    score = jnp.where(forced > 0.0, jnp.inf, jnp.where(blk <= cur, imp, -jnp.inf))
    sel = jnp.zeros(imp.shape, f32)
    for _ in range(n_top):
        mx = jnp.max(score, axis=axis, keepdims=True)
        first = jnp.min(jnp.where(score == mx, blk, float(nb)), axis=axis, keepdims=True)
        hit = blk == first
        sel = jnp.where(hit, 1.0, sel)
        score = jnp.where(hit, -jnp.inf, score)
    return sel


def _nsa_prompt_kernel(q_ref, gate_ref, kc_ref, vc_ref, ks_ref, vs_ref, kw_ref, vw_ref, c2s_ref, kblk_ref, wbias_ref,
                       o_ref, *, hg, n_top, tk):
    D = LANES
    QB = q_ref.shape[0]
    T = ks_ref.shape[0]
    nc = kc_ref.shape[1]
    R = hg * QB
    c_exp = (D ** -0.5) * math.log2(math.e)
    qb = pl.program_id(1)
    s0 = qb * QB
    q = jnp.concatenate([q_ref[:, h * D:(h + 1) * D] for h in range(hg)], axis=0)
    t_row = s0 + (lax.broadcasted_iota(jnp.int32, (R, 1), 0) & (QB - 1))

    kc_end = lax.broadcasted_iota(jnp.int32, (1, nc), 1) * CMP_STRIDE + (CMP_LEN - 1)
    s = jnp.where(kc_end <= t_row, _dot_nt(q, kc_ref[0]), NEG)
    e = jnp.exp2((s - jnp.max(s, axis=-1, keepdims=True)) * c_exp)
    row_ok = jnp.where(t_row >= CMP_LEN - 1, 1.0, 0.0)
    inv = row_ok / jnp.maximum(jnp.sum(e, axis=-1, keepdims=True), 1e-30)
    o_c = jnp.dot(e.astype(bf16), vc_ref[0], preferred_element_type=f32) * inv

    wl = wbias_ref.shape[2]
    w0 = pl.multiple_of(jnp.clip(s0 - WINDOW, 0, T - wl), QB)
    bias = wbias_ref[jnp.minimum(qb, wbias_ref.shape[0] - 1)]
    sw = _dot_nt(q, kw_ref[pl.ds(w0, wl), :])
    sw = jnp.concatenate([sw[h * QB:(h + 1) * QB] + bias for h in range(hg)], axis=0)
    ew = jnp.exp2((sw - jnp.max(sw, axis=-1, keepdims=True)) * c_exp)
    o_w = (jnp.dot(ew.astype(bf16), vw_ref[pl.ds(w0, wl), :], preferred_element_type=f32)
           * (1.0 / jnp.maximum(jnp.sum(ew, axis=-1, keepdims=True), 1e-30)))

    p_sum = e[0:QB] * inv[0:QB]
    for h in range(1, hg):
        p_sum = p_sum + e[h * QB:(h + 1) * QB] * inv[h * QB:(h + 1) * QB]
    imp_t = _dot_split(p_sum, c2s_ref[...]).T
    cur_t = ((s0 + lax.broadcasted_iota(jnp.int32, (1, QB), 1)) // SEL_BLOCK).astype(f32)
    not_sel = (1.0 - _select_blocks(imp_t, cur_t, n_top, 0)).T.astype(bf16)

    q_aug = jnp.concatenate([q, jnp.concatenate([not_sel] * hg, axis=0)], axis=1)
    col_pos = lax.broadcasted_iota(jnp.int32, (1, tk), 1)

    def sel_tile(kt, carry, causal):
        m, l, acc = carry
        k0 = pl.multiple_of(kt * tk, tk)
        s = _dot_nt(q_aug, jnp.concatenate([ks_ref[pl.ds(k0, tk), :], kblk_ref[pl.ds(k0, tk), :]], axis=1))
        if causal:
            s = jnp.where(col_pos + k0 <= t_row, s, NEG)
        m_new = jnp.maximum(m, jnp.max(s, axis=-1, keepdims=True))
        alpha = jnp.exp2((m - m_new) * c_exp)
        e = jnp.exp2((s - m_new) * c_exp)
        l = alpha * l + jnp.sum(e, axis=-1, keepdims=True)
        acc = alpha * acc + jnp.dot(e.astype(bf16), vs_ref[pl.ds(k0, tk), :], preferred_element_type=f32)
        return m_new, l, acc

    last = (s0 + QB - 1) // tk
    init = (jnp.full((R, 1), NEG, f32), jnp.zeros((R, 1), f32), jnp.zeros((R, D), f32))
    carry = lax.fori_loop(0, last, functools.partial(sel_tile, causal=False), init)
    _, l, acc = sel_tile(last, carry, True)
    o_s = acc * (1.0 / jnp.maximum(l, 1e-30))

    gate = jax.nn.sigmoid(gate_ref[...])
    for h in range(hg):
        rows = slice(h * QB, (h + 1) * QB)
        o = (o_c[rows] * gate[:, 3 * h:3 * h + 1] + o_s[rows] * gate[:, 3 * h + 1:3 * h + 2]
             + o_w[rows] * gate[:, 3 * h + 2:3 * h + 3])
        o_ref[:, h * D:(h + 1) * D] = o.astype(o_ref.dtype)


def _cmp_to_sel(nc_pad, nsel_pad, nc, nsel):
    cs = np.arange(nc_pad)[:, None] * CMP_STRIDE
    bs = np.arange(nsel_pad)[None, :] * SEL_BLOCK
    ov = np.clip(np.minimum(cs + CMP_LEN, bs + SEL_BLOCK) - np.maximum(cs, bs), 0, None).astype(np.float32) / CMP_LEN
    ov[nc:, :] = 0.0
    ov[:, nsel:] = 0.0
    return jnp.asarray(ov, dtype=bf16)


def _key_block_onehot(n_keys, nb, first_block=0):
    blk = first_block + np.arange(n_keys)[:, None] // SEL_BLOCK
    return jnp.asarray(np.where(blk == np.arange(nb)[None, :], NEG, 0.0), dtype=bf16)


def _window_bias(QB, wl, n_off):
    d = (np.arange(n_off)[:, None, None] * QB + np.arange(QB)[None, :, None] - np.arange(wl)[None, None, :])
    return jnp.asarray(np.where((d >= 0) & (d < WINDOW), 0.0, NEG), dtype=f32)


def _nsa_prompt(qn, gates, kc, vc, kvb):
    T = qn.shape[0]
    G, D = N_KV_HEADS, LANES
    hg = N_Q_HEADS // G
    QB = Q_BLOCK
    nc_pad = kc.shape[1]
    nc = (T - CMP_LEN) // CMP_STRIDE + 1
    nsel = -(-T // SEL_BLOCK)
    nb = -(-nsel // LANES) * LANES
    c2s = _cmp_to_sel(nc_pad, nb, nc, nsel)
    kblk = _key_block_onehot(T, nb)
    tk = min(SEL_TILE, T)
    wl = WINDOW + QB
    assert T % tk == 0 and T % QB == 0 and WINDOW % QB == 0 and T >= wl and tk % QB == 0
    wbias = _window_bias(QB, wl, WINDOW // QB + 1)
    est = 2 * (4 * T * D * 2 + T * nb * 2 + wbias.size * 4) + 12 * hg * QB * max(tk, nc_pad, wl) * 4
    return pl.pallas_call(
        functools.partial(_nsa_prompt_kernel, hg=hg, n_top=min(SEL_TOP, nsel), tk=tk),
        grid=(G, T // QB),
        in_specs=[pl.BlockSpec((QB, hg * D), lambda g, i: (i, g)),
                  pl.BlockSpec((QB, LANES), lambda g, i: (i, g)),
                  pl.BlockSpec((1, nc_pad, D), lambda g, i: (g, 0, 0)),
                  pl.BlockSpec((1, nc_pad, D), lambda g, i: (g, 0, 0)),
                  pl.BlockSpec((T, D), lambda g, i: (0, g)),
                  pl.BlockSpec((T, D), functools.partial(lambda g, i, o: (0, o + g), o=G)),
                  pl.BlockSpec((T, D), functools.partial(lambda g, i, o: (0, o + g), o=2 * G)),
                  pl.BlockSpec((T, D), functools.partial(lambda g, i, o: (0, o + g), o=3 * G)),
                  pl.BlockSpec((nc_pad, nb), lambda g, i: (0, 0)),
                  pl.BlockSpec((T, nb), lambda g, i: (0, 0)),
                  pl.BlockSpec(wbias.shape, lambda g, i: (0, 0, 0))],
        out_specs=pl.BlockSpec((QB, hg * D), lambda g, i: (i, g)),
        out_shape=jax.ShapeDtypeStruct((T, N_Q_HEADS * D), bf16),
        compiler_params=_cparams(("parallel", "parallel"), est),
        name="nsa_prompt",
    )(qn, gates, kc, vc, kvb, kvb, kvb, kvb, c2s, kblk, wbias)


def _ret_tables(C, n_valid):
    lg = np.log1p(-np.exp2(-5.0 - np.arange(N_RET_HEADS, dtype=np.float64)))
    i = np.arange(C, dtype=np.float64)
    diff = i[:, None] - i[None, :]
    valid = (i < n_valid)
    dmask = np.where(diff >= 0, np.exp(lg[:, None, None] * np.maximum(diff, 0.0)), 0.0)
    dmask = dmask * (valid[:, None] & valid[None, :])
    q_dec = np.exp(lg[:, None] * (i[None, :] + 1.0)) * valid[None, :]
    k_dec = np.exp(lg[:, None] * (n_valid - 1.0 - i[None, :])) * valid[None, :]
    c_dec = np.exp(lg * n_valid)
    as32 = lambda a: jnp.asarray(a, dtype=f32)
    return (as32(dmask), as32(q_dec[:, :, None]), as32(k_dec[:, :, None]),
            as32(np.broadcast_to(c_dec[:, None, None], (N_RET_HEADS, 1, LANES))))


def _rope_tables(pos, half):
    inv = jnp.power(ROPE_BASE, -jnp.arange(half, dtype=f32) / half)
    ang = pos.astype(f32)[:, None] * inv[None, :]
    return jnp.cos(ang), jnp.sin(ang)


def _rotary(x, cos, sin):
    half = cos.shape[-1]
    x1, x2 = x[:, :half], x[:, half:]
    return jnp.concatenate([x1 * cos - x2 * sin, x1 * sin + x2 * cos], axis=-1)


def _ret_chunk(q_raw, k_raw, v, rg, cos, sin, dmask, q_dec, k_dec, c_dec, S):
    dk = q_raw.shape[-1]
    q = _rotary(q_raw, cos, sin)
    k = _rotary(k_raw, cos, sin) * (dk ** -0.5)
    vb = v.astype(bf16)
    inner = _dot_nt(q.astype(bf16), k.astype(bf16)) * dmask
    o = (jnp.dot(inner.astype(bf16), vb, preferred_element_type=f32)
         + jnp.dot((q * q_dec).astype(bf16), S.astype(bf16), preferred_element_type=f32))
    S_new = S * c_dec + lax.dot_general((k * k_dec).astype(bf16), vb, (((0,), (0,)), ((), ())),
                                        preferred_element_type=f32)
    y = o * lax.rsqrt(jnp.mean(o * o, axis=-1, keepdims=True) + EPS)
    return y * (rg * jax.nn.sigmoid(rg)), S_new


def _ret_prompt_kernel(q_ref, k_ref, v_ref, g_ref, cos_ref, sin_ref, dm_ref, qd_ref, kd_ref, cd_ref,
                       y_ref, s_out_ref, s_ref, *, hp, dk):
    c = pl.program_id(1)

    @pl.when(c == 0)
    def _():
        s_ref[...] = jnp.zeros_like(s_ref)

    for h in range(hp):
        sl = slice(h * dk, (h + 1) * dk)
        y, S_new = _ret_chunk(q_ref[:, sl], k_ref[:, sl], v_ref[:, sl], g_ref[:, sl], cos_ref[...], sin_ref[...],
                              dm_ref[h], qd_ref[h], kd_ref[h], cd_ref[h][:, 0:1], s_ref[h])
        y_ref[:, sl] = y.astype(y_ref.dtype)
        s_ref[h] = S_new

    @pl.when(c == pl.num_programs(1) - 1)
    def _():
        s_out_ref[...] = s_ref[...]


def _ret_prompt(proj, lay, T):
    H = N_RET_HEADS
    dk = lay["ret_dk"]
    C = min(RET_CHUNK, T)
    hp = 2 if H % 2 == 0 else 1
    w = hp * dk
    assert T % C == 0 and all(lay[n] % w == 0 for n in ("rq", "rk", "rv", "rg"))
    cos, sin = _rope_tables(jnp.arange(T), dk // 2)
    dmask, q_dec, k_dec, c_dec = _ret_tables(C, C)
    col = lambda name: functools.partial(lambda h, c, o: (c, o + h), o=lay[name] // w)
    tab = lambda shp: pl.BlockSpec((hp,) + shp, lambda h, c: (h, 0, 0))
    return pl.pallas_call(
        functools.partial(_ret_prompt_kernel, hp=hp, dk=dk),
        grid=(H // hp, T // C),
        in_specs=[pl.BlockSpec((C, w), col("rq")), pl.BlockSpec((C, w), col("rk")),
                  pl.BlockSpec((C, w), col("rv")), pl.BlockSpec((C, w), col("rg")),
                  pl.BlockSpec((C, dk // 2), lambda h, c: (c, 0)), pl.BlockSpec((C, dk // 2), lambda h, c: (c, 0)),
                  tab((C, C)), tab((C, 1)), tab((C, 1)), tab((1, LANES))],
        out_specs=[pl.BlockSpec((C, w), lambda h, c: (c, h)),
                   pl.BlockSpec((hp, dk, dk), lambda h, c: (h, 0, 0))],
        out_shape=[jax.ShapeDtypeStruct((T, H * dk), bf16),
                   jax.ShapeDtypeStruct((H, dk, dk), f32)],
        scratch_shapes=[pltpu.VMEM((hp, dk, dk), f32)],
        compiler_params=_cparams(("parallel", "arbitrary"), hp * (16 * C * dk * 4 + 4 * dk * dk * 4)),
        name="ret_prompt",
    )(proj, proj, proj, proj, cos, sin, dmask, q_dec, k_dec, c_dec)


def _ret_sample_kernel(q_ref, k_ref, v_ref, g_ref, cos_ref, sin_ref, dm_ref, qd_ref, kd_ref, cd_ref, s0_ref,
                       y_ref, s_out_ref, *, n_heads, dk):
    for h in range(n_heads):
        sl = slice(h * dk, (h + 1) * dk)
        y, S_new = _ret_chunk(q_ref[0][:, sl], k_ref[0][:, sl], v_ref[0][:, sl], g_ref[0][:, sl],
                              cos_ref[...], sin_ref[...], dm_ref[h], qd_ref[h], kd_ref[h], cd_ref[h][:, 0:1],
                              s0_ref[0, h])
        y_ref[0, :, sl] = y.astype(y_ref.dtype)
        s_out_ref[0, h] = S_new


def _ret_sample(proj8, lay, state, past_len, n_tok):
    B, P, _ = proj8.shape
    H = N_RET_HEADS
    dk = lay["ret_dk"]
    w = H * dk
    assert all(lay[n] % w == 0 for n in ("rq", "rk", "rv", "rg"))
    cos, sin = _rope_tables(past_len + jnp.arange(P), dk // 2)
    dmask, q_dec, k_dec, c_dec = _ret_tables(P, n_tok)
    col = lambda name: functools.partial(lambda b, o: (b, 0, o), o=lay[name] // w)
    full = lambda a: pl.BlockSpec(a.shape, lambda b: (0,) * a.ndim)
    return pl.pallas_call(
        functools.partial(_ret_sample_kernel, n_heads=H, dk=dk),
        grid=(B,),
        in_specs=[pl.BlockSpec((1, P, w), col("rq")), pl.BlockSpec((1, P, w), col("rk")),
                  pl.BlockSpec((1, P, w), col("rv")), pl.BlockSpec((1, P, w), col("rg")),
                  full(cos), full(sin), full(dmask), full(q_dec), full(k_dec), full(c_dec),
                  pl.BlockSpec((1, H, dk, dk), lambda b: (b, 0, 0, 0))],
        out_specs=[pl.BlockSpec((1, P, w), lambda b: (b, 0, 0)),
                   pl.BlockSpec((1, H, dk, dk), lambda b: (b, 0, 0, 0))],
        out_shape=[jax.ShapeDtypeStruct((B, P, w), bf16),
                   jax.ShapeDtypeStruct((B, H, dk, dk), f32)],
        compiler_params=_cparams(("parallel",), 4 * H * dk * dk * 4 + 16 * P * w * 4),
        name="ret_sample",
    )(proj8, proj8, proj8, proj8, cos, sin, dmask, q_dec, k_dec, c_dec, state)


def _rows2d(r):
    return r.reshape(r.shape[0] * r.shape[1] * r.shape[2], r.shape[3])


def _slab(r2, kind, g, n_rows, pitch):
    return r2[pl.ds(kind * (pitch // 2) + g, n_rows, stride=pitch), :]


HB_PITCH = 24


def _page_half_blocks(pages2, kind, G, page_rows, stage_ref):
    pitch = 2 * G
    n_hb = page_rows // CMP_STRIDE
    for g in range(G):
        for p, r2 in enumerate(pages2):
            slab = _slab(r2, kind, g, page_rows, pitch)
            for m in range(n_hb):
                row0 = ((g * len(pages2) + p) * n_hb + m) * HB_PITCH
                stage_ref[pl.ds(row0, CMP_STRIDE), :] = slab[m * CMP_STRIDE:(m + 1) * CMP_STRIDE]
    n = G * len(pages2) * n_hb
    return jnp.concatenate([stage_ref[pl.ds(j, n, stride=HB_PITCH), :] for j in range(CMP_STRIDE)], axis=1)


def _sample_cmp_kernel(pt_ref, *refs, n_pages, G, hg, n_tok, n_top, past_len, nc):
    D = LANES
    page_rows = refs[0].shape[0]
    pages2 = [_rows2d(r) for r in refs[:n_pages]]
    (q_ref, w1tk, w1bk, pek, b1k, w2k, gk, w1tv, w1bv, pev, b1v, w2v, c2s_ref, rep_ref,
     oc_ref, sel_ref, stage_ref) = refs[n_pages:]
    Hk = _page_half_blocks(pages2, 0, G, page_rows, stage_ref).astype(bf16)
    kc = _head_norm(_compress_rows(Hk, w1tk[...], w1bk[...], pek[...], b1k[...], w2k[...]), gk[...]).astype(bf16)
    Hv = _page_half_blocks(pages2, 1, G, page_rows, stage_ref).astype(bf16)
    vc = _compress_rows(Hv, w1tv[...], w1bv[...], pev[...], b1v[...], w2v[...]).astype(bf16)
    npg = Hk.shape[0] // G
    R = hg * n_tok
    RG = G * R
    nb = c2s_ref.shape[1]
    scale = D ** -0.5
    t_row = past_len + lax.broadcasted_iota(jnp.int32, (RG, 1), 0) % n_tok
    n_idx = lax.broadcasted_iota(jnp.int32, (1, npg), 1)
    vis = jnp.where(n_idx * CMP_STRIDE + (CMP_LEN - 1) <= t_row, 1.0, 0.0) * jnp.where(n_idx < nc, 1.0, 0.0)
    s = jnp.concatenate([_dot_nt(q_ref[0, g], kc[g * npg:(g + 1) * npg]) for g in range(G)], axis=0) * scale
    p = _softmax_masked(s, vis)
    for g in range(G):
        oc_ref[0, g] = jnp.dot(p[g * R:(g + 1) * R].astype(bf16), vc[g * npg:(g + 1) * npg],
                               preferred_element_type=f32)
    imp = _dot_split(_dot_split_left(rep_ref[...], p), c2s_ref[...])
    imp_t = jnp.concatenate([imp, jnp.zeros((LANES - RG, nb), f32)], axis=0).T
    cur_t = ((past_len + lax.broadcasted_iota(jnp.int32, (1, LANES), 1) % n_tok) // SEL_BLOCK).astype(f32)
    sel = _select_blocks(imp_t, cur_t, n_top, 0).T
    for g in range(G):
        sel_ref[0, g] = sel[g * R:(g + 1) * R]


def _dot_split_left(a, b):
    hi = b.astype(bf16)
    r1 = b - hi.astype(f32)
    mid = r1.astype(bf16)
    lo = (r1 - mid.astype(f32)).astype(bf16)
    return (jnp.dot(a, hi, preferred_element_type=f32) + jnp.dot(a, mid, preferred_element_type=f32)
            + jnp.dot(a, lo, preferred_element_type=f32))


def _page_specs(n_pages, page_rows, kind_pair, layer):
    return [pl.BlockSpec((None, None, page_rows, 2, N_KV_HEADS, LANES),
                         functools.partial(lambda b, pt, p: (layer, pt[b, p], 0, kind_pair, 0, 0), p=p))
            for p in range(n_pages)]


def _sample_cmp(cache_kv, layer, page_table, q_s, cwk, cwv, kc_norm_g, n_tok):
    B, n_pages = page_table.shape
    page_rows = cache_kv.shape[2]
    G, D = N_KV_HEADS, LANES
    hg = N_Q_HEADS // G
    R = hg * n_tok
    past_len = n_pages * page_rows
    L = past_len + n_tok
    nc = (L - CMP_LEN) // CMP_STRIDE + 1
    npg = past_len // CMP_STRIDE
    assert nc <= npg and npg % LANES == 0
    nsel = -(-L // SEL_BLOCK)
    nb = -(-nsel // LANES) * LANES
    c2s = _cmp_to_sel(npg, nb, nc, nsel)
    assert G * R <= LANES and R % n_tok == 0
    rr = np.arange(G * R)
    rep = jnp.asarray((rr[:, None] // R == rr[None, :] // R) & (rr[:, None] % n_tok == rr[None, :] % n_tok),
                      dtype=bf16)
    gk = kc_norm_g.reshape(1, D)
    consts = list(cwk) + [gk] + list(cwv) + [c2s, rep]
    full = lambda a: pl.BlockSpec(a.shape, lambda b, pt: (0,) * a.ndim)
    est = 2 * n_pages * page_rows * 2 * G * D * 4 + 2 * sum(_nbytes(a.shape, a.dtype) for a in consts) \
        + 6 * G * npg * CMP_STRIDE * D * 4
    return pl.pallas_call(
        functools.partial(_sample_cmp_kernel, n_pages=n_pages, G=G, hg=hg, n_tok=n_tok,
                          n_top=min(SEL_TOP, nsel), past_len=past_len, nc=nc),
        grid_spec=pltpu.PrefetchScalarGridSpec(
            num_scalar_prefetch=1,
            grid=(B,),
            in_specs=_page_specs(n_pages, page_rows, 0, layer)
            + [pl.BlockSpec((1, G, R, D), lambda b, pt: (b, 0, 0, 0))] + [full(a) for a in consts],
            out_specs=[pl.BlockSpec((1, G, R, D), lambda b, pt: (b, 0, 0, 0)),
                       pl.BlockSpec((1, G, R, nb), lambda b, pt: (b, 0, 0, 0))],
            scratch_shapes=[pltpu.VMEM((G * npg * HB_PITCH, D), f32)]),
        out_shape=[jax.ShapeDtypeStruct((B, G, R, D), f32),
                   jax.ShapeDtypeStruct((B, G, R, nb), f32)],
        compiler_params=_cparams(("parallel",), est + G * npg * HB_PITCH * D * 4),
        name="sample_cmp",
    )(page_table, *([cache_kv] * n_pages), q_s, *consts)


def _sample_attn_kernel(pt_ref, *refs, n_pages, G, hg, n_tok, past_len):
    D = LANES
    PR = refs[0].shape[0]
    pages2 = [_rows2d(r) for r in refs[:n_pages]]
    (q_ref, gate_ref, knew_ref, vnew_ref, kwnew_ref, vwnew_ref, wnew_ref, win_ref, sel_ref, oc_ref, kblk_ref,
     o_ref, nwin_ref) = refs[n_pages:]
    pitch = 2 * G
    R = hg * n_tok
    n_win = win_ref.shape[0]
    win2 = _rows2d(win_ref)
    c_exp = (D ** -0.5) * math.log2(math.e)
    t_row = past_len + lax.broadcasted_iota(jnp.int32, (R, 1), 0) % n_tok
    new_pos = past_len + lax.broadcasted_iota(jnp.int32, (1, PR), 1)
    new_ok = (new_pos <= t_row) & (new_pos < past_len + n_tok)
    wp_old = past_len - n_win + lax.broadcasted_iota(jnp.int32, (1, n_win), 1)
    vis_old = (wp_old <= t_row) & (wp_old > t_row - WINDOW) & (wp_old >= 0)
    vis_wnew = new_ok & (new_pos > t_row - WINDOW)
    zeros_new = jnp.zeros((PR - knew_ref.shape[2], D), bf16)
    gate = jax.nn.sigmoid(gate_ref[0])

    def softmax(s):
        e = jnp.exp2((s - jnp.max(s, axis=-1, keepdims=True)) * c_exp)
        return (e * (1.0 / jnp.maximum(jnp.sum(e, axis=-1, keepdims=True), 1e-30))).astype(bf16)

    for g in range(G):
        q = q_ref[0, g]
        q_aug = jnp.concatenate([q, (1.0 - sel_ref[0, g]).astype(bf16)], axis=1)
        s_parts = []
        for p in range(n_pages):
            kp = _slab(pages2[p], 0, g, PR, pitch).astype(bf16)
            s_parts.append(_dot_nt(q_aug, jnp.concatenate([kp, kblk_ref[p]], axis=1)))
        k_new = jnp.concatenate([knew_ref[0, g], zeros_new], axis=0)
        s_new = _dot_nt(q_aug, jnp.concatenate([k_new, kblk_ref[n_pages]], axis=1))
        s_parts.append(jnp.where(new_ok, s_new, NEG))
        p_s = softmax(jnp.concatenate(s_parts, axis=1))
        v_new = jnp.concatenate([vnew_ref[0, g], zeros_new], axis=0)
        o_s = jnp.dot(p_s[:, n_pages * PR:], v_new, preferred_element_type=f32)
        for p in range(n_pages):
            vp = _slab(pages2[p], 1, g, PR, pitch).astype(bf16)
            o_s = o_s + jnp.dot(p_s[:, p * PR:(p + 1) * PR], vp, preferred_element_type=f32)
        kw_old = _slab(win2, 0, g, n_win, pitch).astype(bf16)
        vw_old = _slab(win2, 1, g, n_win, pitch).astype(bf16)
        kw_new = jnp.concatenate([kwnew_ref[0, g], zeros_new], axis=0)
        vw_new = jnp.concatenate([vwnew_ref[0, g], zeros_new], axis=0)
        p_w = softmax(jnp.concatenate([jnp.where(vis_old, _dot_nt(q, kw_old), NEG),
                                       jnp.where(vis_wnew, _dot_nt(q, kw_new), NEG)], axis=1))
        o_w = (jnp.dot(p_w[:, :n_win], vw_old, preferred_element_type=f32)
               + jnp.dot(p_w[:, n_win:], vw_new, preferred_element_type=f32))
        gg = gate[g]
        o_ref[0, g] = (oc_ref[0, g] * gg[:, 0:1] + o_s * gg[:, 1:2] + o_w * gg[:, 2:3]).astype(o_ref.dtype)
    nwin_ref[0:n_win - n_tok] = win_ref[n_tok:n_win]
    nwin_ref[n_win - n_tok:n_win] = wnew_ref[...]


def _sample_attn(cache_kv, layer, page_table, q_s, gate_s, new_rows, win_new, state_win_kv, sel, o_c, n_tok):
    B, n_pages = page_table.shape
    page_rows = cache_kv.shape[2]
    G, D = N_KV_HEADS, LANES
    hg = N_Q_HEADS // G
    R = hg * n_tok
    past_len = n_pages * page_rows
    n_win = state_win_kv.shape[2]
    nb = sel.shape[-1]
    P8 = new_rows[0].shape[2]
    assert page_rows % SEL_BLOCK == 0 and nb == LANES and n_win >= n_tok
    kblk = jnp.stack([_key_block_onehot(page_rows, nb, (p * page_rows) // SEL_BLOCK) for p in range(n_pages + 1)])
    per_b = lambda shp: pl.BlockSpec((1,) + shp, lambda b, pt: (b,) + (0,) * len(shp))
    win_spec = pl.BlockSpec((None, None, n_win, 2, G, D), lambda b, pt: (layer, b, 0, 0, 0, 0))
    est = 2 * n_pages * page_rows * 2 * G * D * 4 + 4 * n_win * 2 * G * D * 4 + 40 * R * (past_len + page_rows) * 4
    return pl.pallas_call(
        functools.partial(_sample_attn_kernel, n_pages=n_pages, G=G, hg=hg, n_tok=n_tok, past_len=past_len),
        grid_spec=pltpu.PrefetchScalarGridSpec(
            num_scalar_prefetch=1,
            grid=(B,),
            in_specs=_page_specs(n_pages, page_rows, 1, layer)
            + [per_b((G, R, D)), per_b((G, R, LANES))] + [per_b((G, P8, D))] * 4
            + [pl.BlockSpec((None, n_tok, 2, G, D), lambda b, pt: (b, 0, 0, 0, 0)), win_spec,
               per_b((G, R, nb)), per_b((G, R, D)), pl.BlockSpec(kblk.shape, lambda b, pt: (0, 0, 0))],
            out_specs=[per_b((G, R, D)), win_spec]),
        out_shape=[jax.ShapeDtypeStruct((B, G, R, D), bf16),
                   jax.ShapeDtypeStruct(state_win_kv.shape, state_win_kv.dtype)],
        compiler_params=_cparams(("parallel",), est),
        name="sample_attn",
    )(page_table, *([cache_kv] * n_pages), q_s, gate_s, *new_rows, win_new, state_win_kv, sel, o_c, kblk)


def _pick_tile(n, pref):
    t = min(n, pref)
    while n % t:
        t //= 2
    return t


def _proj_layout(ret_w, qw, kvw):
    lay, off = {}, 0
    for name, w in (("rq", ret_w), ("rk", ret_w), ("rv", ret_w), ("rg", ret_w), ("nq", qw), ("nkv", 6 * kvw)):
        lay[name] = off
        off += w
    lay["width"] = off
    lay["ret_dk"] = ret_w // N_RET_HEADS
    return lay


def _prep_w_in(w_in, d_model, n_main):
    G = N_KV_HEADS
    hg = N_Q_HEADS // G
    ng = w_in[:, n_main:n_main + 3 * N_Q_HEADS].reshape(d_model, G, 3 * hg)
    ng = jnp.pad(ng, ((0, 0), (0, 0), (0, LANES - 3 * hg))).reshape(d_model, G * LANES)
    return ((w_in, n_main), (w_in[:, n_main + 3 * N_Q_HEADS:], 2 * d_model), (ng, G * LANES))


def _in_proj(xn, w_in_parts, tm, tag):
    return [_mm([(xn, 0, xn.shape[1], w)], [], _ep_identity, f32, tm, _pick_tile(n, 512), n_out=n,
                name="in_proj_%s%d" % (tag, i)) for i, (w, n) in enumerate(w_in_parts)]


def _dense_tail(x, y_ret, y_nsa, proj_g, wts, tm):
    w_ret_out, w_nsa_out, w_out, norm2_g, w_up, w_down = wts
    M, d_model = x.shape
    d_ff = w_up.shape[1]
    tn = _pick_tile(d_model, 512)
    m = _mm([(y_ret, 0, y_ret.shape[1], w_ret_out), (y_nsa, 0, y_nsa.shape[1], w_nsa_out)],
            [(proj_g, 0), (proj_g, d_model)], _ep_merge, bf16, tm, tn, name="merge")
    h = _mm([(m, 0, d_model, w_out)], [(x, 0)], _ep_residual, f32, tm, tn, name="out_proj")
    hn = _rmsnorm(h, norm2_g, _pick_tile(M, 256))
    u = _mm([(hn, 0, d_model, w_up)], [], _ep_relu2, bf16, tm, _pick_tile(d_ff, 512), name="mlp_up")
    return _mm([(u, 0, d_ff, w_down)], [(h, 0)], _ep_residual, f32, tm, _pick_tile(d_model, 1024),
               tk=_pick_tile(d_ff, 2048), name="mlp_down")


def kernel(x_prompt, x_sample, cache_kv, page_table, state_win_kv, state_ret, norm1_g, w_in, w_ret_out,
           q_norm_g, kc_norm_g, ks_norm_g, kw_norm_g, cmp_pe_k, cmp_w1_k, cmp_b1_k, cmp_w2_k,
           cmp_pe_v, cmp_w1_v, cmp_b1_v, cmp_w2_v, w_nsa_out, w_out, norm2_g, w_up, w_down):
    depth = w_in.shape[0]
    assert depth == 1 and x_prompt.shape[0] == 1
    B, T, d_model = x_prompt.shape
    Bs, Ts, _ = x_sample.shape
    G, D = N_KV_HEADS, LANES
    hg = N_Q_HEADS // G
    qw, kvw = N_Q_HEADS * D, G * D
    ret_w = w_ret_out.shape[1]
    dk = ret_w // N_RET_HEADS
    page_rows = cache_kv.shape[2]
    n_pages = page_table.shape[1]
    past_len = n_pages * page_rows
    lay = _proj_layout(ret_w, qw, kvw)

    l = 0
    w_in_parts = _prep_w_in(w_in[l], d_model, lay["width"])
    wts = (w_ret_out[l], w_nsa_out[l], w_out[l], norm2_g[l], w_up[l], w_down[l])
    cwk = _cmp_weights(cmp_pe_k[l], cmp_w1_k[l], cmp_b1_k[l], cmp_w2_k[l])
    cwv = _cmp_weights(cmp_pe_v[l], cmp_w1_v[l], cmp_b1_v[l], cmp_w2_v[l])

    xp = x_prompt.reshape(T, d_model)
    tm = _pick_tile(T, 1024)
    xn = _rmsnorm(xp, norm1_g[l], _pick_tile(T, 256))
    proj, proj_g, gates = _in_proj(xn, w_in_parts, tm, "p")
    qn, paged, win, kvb = _nsa_prep(proj, lay, q_norm_g[l], ks_norm_g[l], kw_norm_g[l], _pick_tile(T, 256))
    nhb = T // CMP_STRIDE
    half_blocks = (paged[:, :2 * kvw].astype(bf16).reshape(nhb, CMP_STRIDE, 2, G, D)
                   .transpose(2, 3, 0, 1, 4).reshape(2, G, nhb, CMP_STRIDE * D))
    kc = _compress_prompt(half_blocks[0], cwk, kc_norm_g[l], True)
    vc = _compress_prompt(half_blocks[1], cwv, kc_norm_g[l], False)
    y_nsa = _nsa_prompt(qn, gates, kc, vc, kvb)
    y_ret, s_prompt = _ret_prompt(proj, lay, T)
    y_prompt = _dense_tail(xp, y_ret, y_nsa, proj_g, wts, tm).reshape(B, T, d_model)
    new_kv_prompt = paged.reshape(1, B, T, 4, G, D)
    n_keep = min(WINDOW, T)
    new_win_prompt = win[T - n_keep:].reshape(1, B, n_keep, 2, G, D)
    new_ret_prompt = s_prompt.reshape(1, B, N_RET_HEADS, dk, dk)

    Ms = Bs * Ts
    xs = x_sample.reshape(Ms, d_model)
    tms = _pick_tile(Ms, 512)
    xn = _rmsnorm(xs, norm1_g[l], _pick_tile(Ms, 256))
    proj_s, proj_gs, gates = _in_proj(xn, w_in_parts, tms, "s")
    qn, paged_s, win_s, kvb = _nsa_prep(proj_s, lay, q_norm_g[l], ks_norm_g[l], kw_norm_g[l], _pick_tile(Ms, 256))
    P8 = 8
    assert Ts <= P8
    proj8 = jnp.pad(proj_s.reshape(Bs, Ts, lay["width"]), ((0, 0), (0, P8 - Ts), (0, 0)))
    y_ret8, s_sample = _ret_sample(proj8, lay, state_ret[l], past_len, Ts)
    y_ret = y_ret8[:, :Ts].reshape(Ms, ret_w)
    q_s = qn.reshape(Bs, Ts, G, hg, D).transpose(0, 2, 3, 1, 4).reshape(Bs, G, hg * Ts, D)
    gate_s = (gates.reshape(Bs, Ts, G, LANES)[..., :3 * hg]
              .reshape(Bs, Ts, G, hg, 3).transpose(0, 2, 3, 1, 4).reshape(Bs, G, hg * Ts, 3))
    gate_s = jnp.pad(gate_s, ((0, 0), (0, 0), (0, 0), (0, LANES - 3)))
    kvb5 = jnp.pad(kvb.reshape(Bs, Ts, 4, G, D), ((0, 0), (0, P8 - Ts), (0, 0), (0, 0), (0, 0)))
    new_rows = [kvb5[:, :, c].transpose(0, 2, 1, 3) for c in range(4)]
    win_new = win_s.reshape(Bs, Ts, 2, G, D)
    o_c, sel = _sample_cmp(cache_kv, l, page_table, q_s, cwk, cwv, kc_norm_g[l], Ts)
    o_s, new_win_sample = _sample_attn(cache_kv, l, page_table, q_s, gate_s, new_rows, win_new, state_win_kv,
                                       sel, o_c, Ts)
    y_nsa = o_s.reshape(Bs, G, hg, Ts, D).transpose(0, 3, 1, 2, 4).reshape(Ms, qw)
    y_sample = _dense_tail(xs, y_ret, y_nsa, proj_gs, wts, tms).reshape(Bs, Ts, d_model)
    new_kv_sample = paged_s.reshape(1, Bs, Ts, 4, G, D)
    new_ret_sample = s_sample.reshape(1, Bs, N_RET_HEADS, dk, dk)
    return (y_prompt, y_sample, new_kv_prompt, new_win_prompt, new_ret_prompt,
            new_kv_sample, new_win_sample, new_ret_sample)
```

```python
import functools
import math

import numpy as np
import jax
import jax.numpy as jnp
from jax import lax
from jax.experimental import pallas as pl
from jax.experimental.pallas import tpu as pltpu

N_RET_HEADS = 8
RET_CHUNK = 128
ROPE_BASE = 10000.0
N_Q_HEADS = 16
N_KV_HEADS = 4
CMP_LEN = 32
CMP_STRIDE = 16
SEL_BLOCK = 64
SEL_TOP = 16
WINDOW = 512
Q_BLOCK = 256
EPS = 1e-6

LANES = 128
VMEM_BYTES = 64 * 1024 * 1024
NEG = -1e30
SEL_TILE = 1024

f32 = jnp.float32
bf16 = jnp.bfloat16


def _cparams(sem, est_bytes):
    limit = int(min(VMEM_BYTES - (4 << 20), max(est_bytes + (8 << 20), 32 << 20)))
    return pltpu.CompilerParams(dimension_semantics=sem, vmem_limit_bytes=limit)


def _nbytes(shape, dtype):
    return int(np.prod(shape)) * jnp.dtype(dtype).itemsize


def _rmsnorm_kernel(x_ref, g_ref, o_ref):
    x = x_ref[...]
    y = x * lax.rsqrt(jnp.mean(x * x, axis=-1, keepdims=True) + EPS)
    o_ref[...] = (y * g_ref[...]).astype(o_ref.dtype)


def _rmsnorm(x, g, tm):
    M, D = x.shape
    return pl.pallas_call(
        _rmsnorm_kernel,
        grid=(M // tm,),
        in_specs=[pl.BlockSpec((tm, D), lambda i: (i, 0)),
                  pl.BlockSpec((1, D), lambda i: (0, 0))],
        out_specs=pl.BlockSpec((tm, D), lambda i: (i, 0)),
        out_shape=jax.ShapeDtypeStruct((M, D), bf16),
        compiler_params=_cparams(("parallel",), 2 * tm * D * 6),
        name="rmsnorm",
    )(x, g.reshape(1, D))


def _mm_kernel(*refs, n_pairs, n_extra, nk, epilogue, b_rows):
    a = refs[:n_pairs]
    b = refs[n_pairs:2 * n_pairs]
    ex = refs[2 * n_pairs:2 * n_pairs + n_extra]
    o_ref = refs[2 * n_pairs + n_extra]
    if nk == 1:
        dots = [(_dot_nt if b_rows[i] else functools.partial(jnp.dot, preferred_element_type=f32))(
            a[i][...], b[i][...].astype(bf16)) for i in range(n_pairs)]
        o_ref[...] = epilogue(dots, [e[...] for e in ex]).astype(o_ref.dtype)
    else:
        acc_ref = refs[-1]
        k = pl.program_id(2)

        @pl.when(k == 0)
        def _():
            acc_ref[...] = jnp.zeros_like(acc_ref)

        acc_ref[...] += jnp.dot(a[0][...], b[0][...].astype(bf16), preferred_element_type=f32)

        @pl.when(k == nk - 1)
        def _():
            o_ref[...] = epilogue([acc_ref[...]], [e[...] for e in ex]).astype(o_ref.dtype)


def _mm(pairs, extras, epilogue, out_dtype, tm, tn, tk=None, n_out=None, b_row0=None, name="mm"):
    M = pairs[0][0].shape[0]
    N = pairs[0][3].shape[1] if n_out is None else n_out
    K0 = pairs[0][2]
    tk = K0 if tk is None else tk
    nk = K0 // tk
    assert M % tm == 0 and N % tn == 0 and K0 % tk == 0
    assert nk == 1 or (len(pairs) == 1 and b_row0 is None)
    in_specs, args, est = [], [], 0
    for (a, off, K, b) in pairs:
        kb = K if nk == 1 else tk
        assert off % kb == 0
        in_specs.append(pl.BlockSpec((tm, kb), functools.partial(lambda i, j, k, o: (i, o + k), o=off // kb)))
        args.append(a)
        est += 2 * _nbytes((tm, kb), a.dtype)
    for (a, off, K, b) in pairs:
        kb = K if nk == 1 else tk
        if b_row0 is None:
            in_specs.append(pl.BlockSpec((kb, tn), lambda i, j, k: (k, j)))
        else:
            assert b_row0 % 8 == 0 and tn % 8 == 0
            in_specs.append(pl.BlockSpec((pl.Element(tn), pl.Element(kb)),
                                         lambda i, j, k: (pl.multiple_of(b_row0 + j * tn, 8), 0)))
        args.append(b)
        est += 2 * _nbytes((kb, tn), b.dtype) + (0 if b.dtype == bf16 else _nbytes((kb, tn), bf16))
    for (e, off) in extras:
        assert off % tn == 0
        in_specs.append(pl.BlockSpec((tm, tn), functools.partial(lambda i, j, k, o: (i, o + j), o=off // tn)))
        args.append(e)
        est += 2 * _nbytes((tm, tn), e.dtype)
    est += 2 * _nbytes((tm, tn), out_dtype) + (len(pairs) + 2) * tm * tn * 4
    scratch = [pltpu.VMEM((tm, tn), f32)] if nk > 1 else []
    return pl.pallas_call(
        functools.partial(_mm_kernel, n_pairs=len(pairs), n_extra=len(extras), nk=nk, epilogue=epilogue,
                          b_rows=[b_row0 is not None] * len(pairs)),
        grid=(M // tm, N // tn, nk),
        in_specs=in_specs,
        out_specs=pl.BlockSpec((tm, tn), lambda i, j, k: (i, j)),
        out_shape=jax.ShapeDtypeStruct((M, N), out_dtype),
        scratch_shapes=scratch,
        compiler_params=_cparams(("parallel", "parallel", "arbitrary"), est),
        name=name,
    )(*args)


def _ep_identity(dots, ex):
    return dots[0]


def _ep_merge(dots, ex):
    return jax.nn.sigmoid(ex[0]) * dots[0] + jax.nn.sigmoid(ex[1]) * dots[1]


def _ep_residual(dots, ex):
    return ex[0] + dots[0]


def _ep_relu2(dots, ex):
    return jnp.square(jnp.maximum(dots[0], 0.0))


def _head_norm(x, g):
    return x * lax.rsqrt(jnp.mean(x * x, axis=-1, keepdims=True) + EPS) * g


def _nsa_prep_kernel(q_ref, kv01_ref, kv23_ref, kv45_ref, qg_ref, ksg_ref, kwg_ref,
                     qn_ref, paged_ref, win_ref, kvb_ref, *, n_q, n_kv):
    D = LANES
    kvw = n_kv * D
    for h in range(n_q):
        sl = slice(h * D, (h + 1) * D)
        qn_ref[:, sl] = _head_norm(q_ref[:, sl], qg_ref[...]).astype(qn_ref.dtype)
    paged_ref[:, 0:2 * kvw] = kv01_ref[...]
    for g in range(n_kv):
        sl = slice(g * D, (g + 1) * D)
        ks = _head_norm(kv23_ref[:, sl], ksg_ref[...])
        paged_ref[:, 2 * kvw + g * D:2 * kvw + (g + 1) * D] = ks
        kvb_ref[:, sl] = ks.astype(kvb_ref.dtype)
        kw = _head_norm(kv45_ref[:, sl], kwg_ref[...])
        win_ref[:, sl] = kw
        kvb_ref[:, 2 * kvw + g * D:2 * kvw + (g + 1) * D] = kw.astype(kvb_ref.dtype)
    vs = kv23_ref[:, kvw:2 * kvw]
    paged_ref[:, 3 * kvw:4 * kvw] = vs
    kvb_ref[:, kvw:2 * kvw] = vs.astype(kvb_ref.dtype)
    vw = kv45_ref[:, kvw:2 * kvw]
    win_ref[:, kvw:2 * kvw] = vw
    kvb_ref[:, 3 * kvw:4 * kvw] = vw.astype(kvb_ref.dtype)


def _nsa_prep(proj, lay, q_norm_g, ks_norm_g, kw_norm_g, tm):
    M = proj.shape[0]
    n_q, n_kv, D = N_Q_HEADS, N_KV_HEADS, LANES
    qw, kvw = n_q * D, n_kv * D
    assert lay["nq"] % qw == 0 and lay["nkv"] % (2 * kvw) == 0
    kvb0 = lay["nkv"] // (2 * kvw)
    gspec = pl.BlockSpec((1, D), lambda i: (0, 0))
    return pl.pallas_call(
        functools.partial(_nsa_prep_kernel, n_q=n_q, n_kv=n_kv),
        grid=(M // tm,),
        in_specs=[pl.BlockSpec((tm, qw), functools.partial(lambda i, o: (i, o), o=lay["nq"] // qw)),
                  pl.BlockSpec((tm, 2 * kvw), functools.partial(lambda i, o: (i, o), o=kvb0)),
                  pl.BlockSpec((tm, 2 * kvw), functools.partial(lambda i, o: (i, o), o=kvb0 + 1)),
                  pl.BlockSpec((tm, 2 * kvw), functools.partial(lambda i, o: (i, o), o=kvb0 + 2)),
                  gspec, gspec, gspec],
        out_specs=[pl.BlockSpec((tm, qw), lambda i: (i, 0)),
                   pl.BlockSpec((tm, 4 * kvw), lambda i: (i, 0)),
                   pl.BlockSpec((tm, 2 * kvw), lambda i: (i, 0)),
                   pl.BlockSpec((tm, 4 * kvw), lambda i: (i, 0))],
        out_shape=[jax.ShapeDtypeStruct((M, qw), bf16),
                   jax.ShapeDtypeStruct((M, 4 * kvw), f32),
                   jax.ShapeDtypeStruct((M, 2 * kvw), f32),
                   jax.ShapeDtypeStruct((M, 4 * kvw), bf16)],
        compiler_params=_cparams(("parallel",), 2 * tm * (qw * 6 + kvw * 6 * 4 + kvw * 6 * 4 + kvw * 8)),
        name="nsa_prep",
    )(proj, proj, proj, proj, q_norm_g.reshape(1, D), ks_norm_g.reshape(1, D), kw_norm_g.reshape(1, D))


def _gelu_tanh(x):
    return 0.5 * x * (1.0 + jnp.tanh(math.sqrt(2.0 / math.pi) * (x + 0.044715 * (x * x * x))))


def _compress_rows(H, w1t, w1b, pe, b1, w2):
    half = w1t.shape[0]
    P = jnp.dot(H, w1t, preferred_element_type=f32)
    Q = jnp.dot(H, w1b, preferred_element_type=f32)
    bias = (jnp.dot(pe[:, :half], w1t, preferred_element_type=f32)
            + jnp.dot(pe[:, half:], w1b, preferred_element_type=f32))[0:1] + b1
    hidden = P + pltpu.roll(Q, Q.shape[0] - 1, 0) + bias
    return jnp.dot(_gelu_tanh(hidden).astype(bf16), w2, preferred_element_type=f32)


def _compress_kernel(h_ref, w1t_ref, w1b_ref, pe_ref, b1_ref, w2_ref, g_ref, o_ref, *, normalize):
    out = _compress_rows(h_ref[0], w1t_ref[...], w1b_ref[...], pe_ref[...], b1_ref[...], w2_ref[...])
    if normalize:
        out = _head_norm(out, g_ref[...])
    o_ref[0] = out.astype(o_ref.dtype)


def _cmp_weights(pe, w1, b1, w2):
    half = w1.shape[0] // 2
    pe8 = jnp.broadcast_to(pe.reshape(1, -1), (8, pe.size)).astype(bf16)
    return (w1[:half].astype(bf16), w1[half:].astype(bf16), pe8, b1.reshape(1, -1).astype(f32), w2.astype(bf16))


def _compress_prompt(H, cwk, g, normalize):
    G, n, W = H.shape
    w1t, w1b, pe8, b1, w2 = cwk
    hid, D = w2.shape
    full = lambda a: pl.BlockSpec(a.shape, lambda i: (0,) * a.ndim)
    g2 = g.reshape(1, D)
    return pl.pallas_call(
        functools.partial(_compress_kernel, normalize=normalize),
        grid=(G,),
        in_specs=[pl.BlockSpec((1, n, W), lambda i: (i, 0, 0)), full(w1t), full(w1b), full(pe8), full(b1), full(w2),
                  full(g2)],
        out_specs=pl.BlockSpec((1, n, D), lambda i: (i, 0, 0)),
        out_shape=jax.ShapeDtypeStruct((G, n, D), bf16),
        compiler_params=_cparams(("parallel",), 2 * (n * W * 2 + 2 * W * hid * 2) + 6 * n * hid * 4),
        name="compress_prompt",
    )(H, w1t, w1b, pe8, b1, w2, g2)


def _dot_nt(a, b):
    return lax.dot_general(a, b, (((1,), (1,)), ((), ())), preferred_element_type=f32)


def _dot_split(a, b):
    hi = a.astype(bf16)
    r1 = a - hi.astype(f32)
    mid = r1.astype(bf16)
    lo = (r1 - mid.astype(f32)).astype(bf16)
    return (jnp.dot(hi, b, preferred_element_type=f32) + jnp.dot(mid, b, preferred_element_type=f32)
            + jnp.dot(lo, b, preferred_element_type=f32))


def _softmax_masked(s, maskf):
    vis = maskf > 0.0
    sm = jnp.where(vis, s, NEG)
    m = jnp.max(sm, axis=-1, keepdims=True)
    e = jnp.where(vis, jnp.exp(sm - m), 0.0)
    return e / jnp.maximum(jnp.sum(e, axis=-1, keepdims=True), 1e-30)


def _select_blocks(imp, cur, n_top, axis):
    nb = imp.shape[axis]
    blk = lax.broadcasted_iota(jnp.int32, imp.shape, axis).astype(f32)
    forced = jnp.where(blk == 0.0, 1.0, 0.0) + jnp.where(blk == cur, 1.0, 0.0) + jnp.where(blk == cur - 1.0, 1.0, 0.0)
    score = jnp.where(forced > 0.0, jnp.inf, jnp.where(blk <= cur, imp, -jnp.inf))
    sel = jnp.zeros(imp.shape, f32)
    for _ in range(n_top):
        mx = jnp.max(score, axis=axis, keepdims=True)
        first = jnp.min(jnp.where(score == mx, blk, float(nb)), axis=axis, keepdims=True)
        hit = blk == first
        sel = jnp.where(hit, 1.0, sel)
        score = jnp.where(hit, -jnp.inf, score)
    return sel


def _nsa_prompt_kernel(q_ref, gate_ref, kc_ref, vc_ref, ks_ref, vs_ref, kw_ref, vw_ref, c2s_ref, kblk_ref, wbias_ref,
                       o_ref, *, hg, n_top, tk):
    D = LANES
    QB = q_ref.shape[0]
    T = ks_ref.shape[0]
    nc = kc_ref.shape[1]
    R = hg * QB
    c_exp = (D ** -0.5) * math.log2(math.e)
    qb = pl.program_id(1)
    s0 = qb * QB
    q = jnp.concatenate([q_ref[:, h * D:(h + 1) * D] for h in range(hg)], axis=0)
    t_row = s0 + (lax.broadcasted_iota(jnp.int32, (R, 1), 0) & (QB - 1))

    kc_end = lax.broadcasted_iota(jnp.int32, (1, nc), 1) * CMP_STRIDE + (CMP_LEN - 1)
    s = jnp.where(kc_end <= t_row, _dot_nt(q, kc_ref[0]), NEG)
    e = jnp.exp2((s - jnp.max(s, axis=-1, keepdims=True)) * c_exp)
    row_ok = jnp.where(t_row >= CMP_LEN - 1, 1.0, 0.0)
    inv = row_ok / jnp.maximum(jnp.sum(e, axis=-1, keepdims=True), 1e-30)
    o_c = jnp.dot(e.astype(bf16), vc_ref[0], preferred_element_type=f32) * inv

    wl = wbias_ref.shape[2]
    w0 = pl.multiple_of(jnp.clip(s0 - WINDOW, 0, T - wl), QB)
    bias = wbias_ref[jnp.minimum(qb, wbias_ref.shape[0] - 1)]
    sw = _dot_nt(q, kw_ref[pl.ds(w0, wl), :])
    sw = jnp.concatenate([sw[h * QB:(h + 1) * QB] + bias for h in range(hg)], axis=0)
    ew = jnp.exp2((sw - jnp.max(sw, axis=-1, keepdims=True)) * c_exp)
    o_w = (jnp.dot(ew.astype(bf16), vw_ref[pl.ds(w0, wl), :], preferred_element_type=f32)
           * (1.0 / jnp.maximum(jnp.sum(ew, axis=-1, keepdims=True), 1e-30)))

    p_sum = e[0:QB] * inv[0:QB]
    for h in range(1, hg):
        p_sum = p_sum + e[h * QB:(h + 1) * QB] * inv[h * QB:(h + 1) * QB]
    imp_t = _dot_split(p_sum, c2s_ref[...]).T
    cur_t = ((s0 + lax.broadcasted_iota(jnp.int32, (1, QB), 1)) // SEL_BLOCK).astype(f32)
    not_sel = (1.0 - _select_blocks(imp_t, cur_t, n_top, 0)).T.astype(bf16)

    q_aug = jnp.concatenate([q, jnp.concatenate([not_sel] * hg, axis=0)], axis=1)
    col_pos = lax.broadcasted_iota(jnp.int32, (1, tk), 1)

    def sel_tile(kt, carry, causal):
        m, l, acc = carry
        k0 = pl.multiple_of(kt * tk, tk)
        s = _dot_nt(q_aug, jnp.concatenate([ks_ref[pl.ds(k0, tk), :], kblk_ref[pl.ds(k0, tk), :]], axis=1))
        if causal:
            s = jnp.where(col_pos + k0 <= t_row, s, NEG)
        m_new = jnp.maximum(m, jnp.max(s, axis=-1, keepdims=True))
        alpha = jnp.exp2((m - m_new) * c_exp)
        e = jnp.exp2((s - m_new) * c_exp)
        l = alpha * l + jnp.sum(e, axis=-1, keepdims=True)
        acc = alpha * acc + jnp.dot(e.astype(bf16), vs_ref[pl.ds(k0, tk), :], preferred_element_type=f32)
        return m_new, l, acc

    last = (s0 + QB - 1) // tk
    init = (jnp.full((R, 1), NEG, f32), jnp.zeros((R, 1), f32), jnp.zeros((R, D), f32))
    carry = lax.fori_loop(0, last, functools.partial(sel_tile, causal=False), init)
    _, l, acc = sel_tile(last, carry, True)
    o_s = acc * (1.0 / jnp.maximum(l, 1e-30))

    gate = jax.nn.sigmoid(gate_ref[...])
    for h in range(hg):
        rows = slice(h * QB, (h + 1) * QB)
        o = (o_c[rows] * gate[:, 3 * h:3 * h + 1] + o_s[rows] * gate[:, 3 * h + 1:3 * h + 2]
             + o_w[rows] * gate[:, 3 * h + 2:3 * h + 3])
        o_ref[:, h * D:(h + 1) * D] = o.astype(o_ref.dtype)


def _cmp_to_sel(nc_pad, nsel_pad, nc, nsel):
    cs = np.arange(nc_pad)[:, None] * CMP_STRIDE
    bs = np.arange(nsel_pad)[None, :] * SEL_BLOCK
    ov = np.clip(np.minimum(cs + CMP_LEN, bs + SEL_BLOCK) - np.maximum(cs, bs), 0, None).astype(np.float32) / CMP_LEN
    ov[nc:, :] = 0.0
    ov[:, nsel:] = 0.0
    return jnp.asarray(ov, dtype=bf16)


def _key_block_onehot(n_keys, nb, first_block=0):
    blk = first_block + np.arange(n_keys)[:, None] // SEL_BLOCK
    return jnp.asarray(np.where(blk == np.arange(nb)[None, :], NEG, 0.0), dtype=bf16)


def _window_bias(QB, wl, n_off):
    d = (np.arange(n_off)[:, None, None] * QB + np.arange(QB)[None, :, None] - np.arange(wl)[None, None, :])
    return jnp.asarray(np.where((d >= 0) & (d < WINDOW), 0.0, NEG), dtype=f32)


def _nsa_prompt(qn, gates, kc, vc, kvb):
    T = qn.shape[0]
    G, D = N_KV_HEADS, LANES
    hg = N_Q_HEADS // G
    QB = Q_BLOCK
    nc_pad = kc.shape[1]
    nc = (T - CMP_LEN) // CMP_STRIDE + 1
    nsel = -(-T // SEL_BLOCK)
    nb = -(-nsel // LANES) * LANES
    c2s = _cmp_to_sel(nc_pad, nb, nc, nsel)
    kblk = _key_block_onehot(T, nb)
    tk = min(SEL_TILE, T)
    wl = WINDOW + QB
    assert T % tk == 0 and T % QB == 0 and WINDOW % QB == 0 and T >= wl and tk % QB == 0
    wbias = _window_bias(QB, wl, WINDOW // QB + 1)
    est = 2 * (4 * T * D * 2 + T * nb * 2 + wbias.size * 4) + 12 * hg * QB * max(tk, nc_pad, wl) * 4
    return pl.pallas_call(
        functools.partial(_nsa_prompt_kernel, hg=hg, n_top=min(SEL_TOP, nsel), tk=tk),
        grid=(G, T // QB),
        in_specs=[pl.BlockSpec((QB, hg * D), lambda g, i: (i, g)),
                  pl.BlockSpec((QB, LANES), lambda g, i: (i, g)),
                  pl.BlockSpec((1, nc_pad, D), lambda g, i: (g, 0, 0)),
                  pl.BlockSpec((1, nc_pad, D), lambda g, i: (g, 0, 0)),
                  pl.BlockSpec((T, D), lambda g, i: (0, g)),
                  pl.BlockSpec((T, D), functools.partial(lambda g, i, o: (0, o + g), o=G)),
                  pl.BlockSpec((T, D), functools.partial(lambda g, i, o: (0, o + g), o=2 * G)),
                  pl.BlockSpec((T, D), functools.partial(lambda g, i, o: (0, o + g), o=3 * G)),
                  pl.BlockSpec((nc_pad, nb), lambda g, i: (0, 0)),
                  pl.BlockSpec((T, nb), lambda g, i: (0, 0)),
                  pl.BlockSpec(wbias.shape, lambda g, i: (0, 0, 0))],
        out_specs=pl.BlockSpec((QB, hg * D), lambda g, i: (i, g)),
        out_shape=jax.ShapeDtypeStruct((T, N_Q_HEADS * D), bf16),
        compiler_params=_cparams(("parallel", "parallel"), est),
        name="nsa_prompt",
    )(qn, gates, kc, vc, kvb, kvb, kvb, kvb, c2s, kblk, wbias)


def _ret_tables(C, n_valid):
    lg = np.log1p(-np.exp2(-5.0 - np.arange(N_RET_HEADS, dtype=np.float64)))
    i = np.arange(C, dtype=np.float64)
    diff = i[:, None] - i[None, :]
    valid = (i < n_valid)
    dmask = np.where(diff >= 0, np.exp(lg[:, None, None] * np.maximum(diff, 0.0)), 0.0)
    dmask = dmask * (valid[:, None] & valid[None, :])
    q_dec = np.exp(lg[:, None] * (i[None, :] + 1.0)) * valid[None, :]
    k_dec = np.exp(lg[:, None] * (n_valid - 1.0 - i[None, :])) * valid[None, :]
    c_dec = np.exp(lg * n_valid)
    as32 = lambda a: jnp.asarray(a, dtype=f32)
    return (as32(dmask), as32(q_dec[:, :, None]), as32(k_dec[:, :, None]),
            as32(np.broadcast_to(c_dec[:, None, None], (N_RET_HEADS, 1, LANES))))


def _rope_tables(pos, half):
    inv = jnp.power(ROPE_BASE, -jnp.arange(half, dtype=f32) / half)
    ang = pos.astype(f32)[:, None] * inv[None, :]
    return jnp.cos(ang), jnp.sin(ang)


def _rotary(x, cos, sin):
    half = cos.shape[-1]
    x1, x2 = x[:, :half], x[:, half:]
    return jnp.concatenate([x1 * cos - x2 * sin, x1 * sin + x2 * cos], axis=-1)


def _ret_chunk(q_raw, k_raw, v, rg, cos, sin, dmask, q_dec, k_dec, c_dec, S):
    dk = q_raw.shape[-1]
    q = _rotary(q_raw, cos, sin)
    k = _rotary(k_raw, cos, sin) * (dk ** -0.5)
    vb = v.astype(bf16)
    inner = _dot_nt(q.astype(bf16), k.astype(bf16)) * dmask
    o = (jnp.dot(inner.astype(bf16), vb, preferred_element_type=f32)
         + jnp.dot((q * q_dec).astype(bf16), S.astype(bf16), preferred_element_type=f32))
    S_new = S * c_dec + lax.dot_general((k * k_dec).astype(bf16), vb, (((0,), (0,)), ((), ())),
                                        preferred_element_type=f32)
    y = o * lax.rsqrt(jnp.mean(o * o, axis=-1, keepdims=True) + EPS)
    return y * (rg * jax.nn.sigmoid(rg)), S_new


def _ret_prompt_kernel(q_ref, k_ref, v_ref, g_ref, cos_ref, sin_ref, dm_ref, qd_ref, kd_ref, cd_ref,
                       y_ref, s_out_ref, s_ref, *, hp, dk):
    c = pl.program_id(1)

    @pl.when(c == 0)
    def _():
        s_ref[...] = jnp.zeros_like(s_ref)

    for h in range(hp):
        sl = slice(h * dk, (h + 1) * dk)
        y, S_new = _ret_chunk(q_ref[:, sl], k_ref[:, sl], v_ref[:, sl], g_ref[:, sl], cos_ref[...], sin_ref[...],
                              dm_ref[h], qd_ref[h], kd_ref[h], cd_ref[h][:, 0:1], s_ref[h])
        y_ref[:, sl] = y.astype(y_ref.dtype)
        s_ref[h] = S_new

    @pl.when(c == pl.num_programs(1) - 1)
    def _():
        s_out_ref[...] = s_ref[...]


def _ret_prompt(proj, lay, T):
    H = N_RET_HEADS
    dk = lay["ret_dk"]
    C = min(RET_CHUNK, T)
    hp = 2 if H % 2 == 0 else 1
    w = hp * dk
    assert T % C == 0 and all(lay[n] % w == 0 for n in ("rq", "rk", "rv", "rg"))
    cos, sin = _rope_tables(jnp.arange(T), dk // 2)
    dmask, q_dec, k_dec, c_dec = _ret_tables(C, C)
    col = lambda name: functools.partial(lambda h, c, o: (c, o + h), o=lay[name] // w)
    tab = lambda shp: pl.BlockSpec((hp,) + shp, lambda h, c: (h, 0, 0))
    return pl.pallas_call(
        functools.partial(_ret_prompt_kernel, hp=hp, dk=dk),
        grid=(H // hp, T // C),
        in_specs=[pl.BlockSpec((C, w), col("rq")), pl.BlockSpec((C, w), col("rk")),
                  pl.BlockSpec((C, w), col("rv")), pl.BlockSpec((C, w), col("rg")),
                  pl.BlockSpec((C, dk // 2), lambda h, c: (c, 0)), pl.BlockSpec((C, dk // 2), lambda h, c: (c, 0)),
                  tab((C, C)), tab((C, 1)), tab((C, 1)), tab((1, LANES))],
        out_specs=[pl.BlockSpec((C, w), lambda h, c: (c, h)),
                   pl.BlockSpec((hp, dk, dk), lambda h, c: (h, 0, 0))],
        out_shape=[jax.ShapeDtypeStruct((T, H * dk), bf16),
                   jax.ShapeDtypeStruct((H, dk, dk), f32)],
        scratch_shapes=[pltpu.VMEM((hp, dk, dk), f32)],
        compiler_params=_cparams(("parallel", "arbitrary"), hp * (16 * C * dk * 4 + 4 * dk * dk * 4)),
        name="ret_prompt",
    )(proj, proj, proj, proj, cos, sin, dmask, q_dec, k_dec, c_dec)


def _ret_sample_kernel(q_ref, k_ref, v_ref, g_ref, cos_ref, sin_ref, dm_ref, qd_ref, kd_ref, cd_ref, s0_ref,
                       y_ref, s_out_ref, *, n_heads, dk):
    for h in range(n_heads):
        sl = slice(h * dk, (h + 1) * dk)
        y, S_new = _ret_chunk(q_ref[0][:, sl], k_ref[0][:, sl], v_ref[0][:, sl], g_ref[0][:, sl],
                              cos_ref[...], sin_ref[...], dm_ref[h], qd_ref[h], kd_ref[h], cd_ref[h][:, 0:1],
                              s0_ref[0, h])
        y_ref[0, :, sl] = y.astype(y_ref.dtype)
        s_out_ref[0, h] = S_new


def _ret_sample(proj8, lay, state, past_len, n_tok):
    B, P, _ = proj8.shape
    H = N_RET_HEADS
    dk = lay["ret_dk"]
    w = H * dk
    assert all(lay[n] % w == 0 for n in ("rq", "rk", "rv", "rg"))
    cos, sin = _rope_tables(past_len + jnp.arange(P), dk // 2)
    dmask, q_dec, k_dec, c_dec = _ret_tables(P, n_tok)
    col = lambda name: functools.partial(lambda b, o: (b, 0, o), o=lay[name] // w)
    full = lambda a: pl.BlockSpec(a.shape, lambda b: (0,) * a.ndim)
    return pl.pallas_call(
        functools.partial(_ret_sample_kernel, n_heads=H, dk=dk),
        grid=(B,),
        in_specs=[pl.BlockSpec((1, P, w), col("rq")), pl.BlockSpec((1, P, w), col("rk")),
                  pl.BlockSpec((1, P, w), col("rv")), pl.BlockSpec((1, P, w), col("rg")),
                  full(cos), full(sin), full(dmask), full(q_dec), full(k_dec), full(c_dec),
                  pl.BlockSpec((1, H, dk, dk), lambda b: (b, 0, 0, 0))],
        out_specs=[pl.BlockSpec((1, P, w), lambda b: (b, 0, 0)),
                   pl.BlockSpec((1, H, dk, dk), lambda b: (b, 0, 0, 0))],
        out_shape=[jax.ShapeDtypeStruct((B, P, w), bf16),
                   jax.ShapeDtypeStruct((B, H, dk, dk), f32)],
        compiler_params=_cparams(("parallel",), 4 * H * dk * dk * 4 + 16 * P * w * 4),
        name="ret_sample",
    )(proj8, proj8, proj8, proj8, cos, sin, dmask, q_dec, k_dec, c_dec, state)


def _rows2d(r):
    return r.reshape(r.shape[0] * r.shape[1] * r.shape[2], r.shape[3])


def _slab(r2, kind, g, n_rows, pitch):
    return r2[pl.ds(kind * (pitch // 2) + g, n_rows, stride=pitch), :]


HB_PITCH = 24


def _page_half_blocks(pages2, kind, G, page_rows, stage_ref):
    pitch = 2 * G
    n_hb = page_rows // CMP_STRIDE
    for g in range(G):
        for p, r2 in enumerate(pages2):
            slab = _slab(r2, kind, g, page_rows, pitch)
            for m in range(n_hb):
                row0 = ((g * len(pages2) + p) * n_hb + m) * HB_PITCH
                stage_ref[pl.ds(row0, CMP_STRIDE), :] = slab[m * CMP_STRIDE:(m + 1) * CMP_STRIDE]
    n = G * len(pages2) * n_hb
    return jnp.concatenate([stage_ref[pl.ds(j, n, stride=HB_PITCH), :] for j in range(CMP_STRIDE)], axis=1)


def _sample_cmp_kernel(pt_ref, *refs, n_pages, G, hg, n_tok, n_top, past_len, nc):
    D = LANES
    page_rows = refs[0].shape[0]
    pages2 = [_rows2d(r) for r in refs[:n_pages]]
    (q_ref, w1tk, w1bk, pek, b1k, w2k, gk, w1tv, w1bv, pev, b1v, w2v, c2s_ref, rep_ref,
     oc_ref, sel_ref, stage_ref) = refs[n_pages:]
    Hk = _page_half_blocks(pages2, 0, G, page_rows, stage_ref).astype(bf16)
    kc = _head_norm(_compress_rows(Hk, w1tk[...], w1bk[...], pek[...], b1k[...], w2k[...]), gk[...]).astype(bf16)
    Hv = _page_half_blocks(pages2, 1, G, page_rows, stage_ref).astype(bf16)
    vc = _compress_rows(Hv, w1tv[...], w1bv[...], pev[...], b1v[...], w2v[...]).astype(bf16)
    npg = Hk.shape[0] // G
    R = hg * n_tok
    RG = G * R
    nb = c2s_ref.shape[1]
    scale = D ** -0.5
    t_row = past_len + lax.broadcasted_iota(jnp.int32, (RG, 1), 0) % n_tok
    n_idx = lax.broadcasted_iota(jnp.int32, (1, npg), 1)
    vis = jnp.where(n_idx * CMP_STRIDE + (CMP_LEN - 1) <= t_row, 1.0, 0.0) * jnp.where(n_idx < nc, 1.0, 0.0)
    s = jnp.concatenate([_dot_nt(q_ref[0, g], kc[g * npg:(g + 1) * npg]) for g in range(G)], axis=0) * scale
    p = _softmax_masked(s, vis)
    for g in range(G):
        oc_ref[0, g] = jnp.dot(p[g * R:(g + 1) * R].astype(bf16), vc[g * npg:(g + 1) * npg],
                               preferred_element_type=f32)
    imp = _dot_split(_dot_split_left(rep_ref[...], p), c2s_ref[...])
    imp_t = jnp.concatenate([imp, jnp.zeros((LANES - RG, nb), f32)], axis=0).T
    cur_t = ((past_len + lax.broadcasted_iota(jnp.int32, (1, LANES), 1) % n_tok) // SEL_BLOCK).astype(f32)
    sel = _select_blocks(imp_t, cur_t, n_top, 0).T
    for g in range(G):
        sel_ref[0, g] = sel[g * R:(g + 1) * R]


def _dot_split_left(a, b):
    hi = b.astype(bf16)
    r1 = b - hi.astype(f32)
    mid = r1.astype(bf16)
    lo = (r1 - mid.astype(f32)).astype(bf16)
    return (jnp.dot(a, hi, preferred_element_type=f32) + jnp.dot(a, mid, preferred_element_type=f32)
            + jnp.dot(a, lo, preferred_element_type=f32))


def _page_specs(n_pages, page_rows, kind_pair, layer):
    return [pl.BlockSpec((None, None, page_rows, 2, N_KV_HEADS, LANES),
                         functools.partial(lambda b, pt, p: (layer, pt[b, p], 0, kind_pair, 0, 0), p=p))
            for p in range(n_pages)]


def _sample_cmp(cache_kv, layer, page_table, q_s, cwk, cwv, kc_norm_g, n_tok):
    B, n_pages = page_table.shape
    page_rows = cache_kv.shape[2]
    G, D = N_KV_HEADS, LANES
    hg = N_Q_HEADS // G
    R = hg * n_tok
    past_len = n_pages * page_rows
    L = past_len + n_tok
    nc = (L - CMP_LEN) // CMP_STRIDE + 1
    npg = past_len // CMP_STRIDE
    assert nc <= npg and npg % LANES == 0
    nsel = -(-L // SEL_BLOCK)
    nb = -(-nsel // LANES) * LANES
    c2s = _cmp_to_sel(npg, nb, nc, nsel)
    assert G * R <= LANES and R % n_tok == 0
    rr = np.arange(G * R)
    rep = jnp.asarray((rr[:, None] // R == rr[None, :] // R) & (rr[:, None] % n_tok == rr[None, :] % n_tok),
                      dtype=bf16)
    gk = kc_norm_g.reshape(1, D)
    consts = list(cwk) + [gk] + list(cwv) + [c2s, rep]
    full = lambda a: pl.BlockSpec(a.shape, lambda b, pt: (0,) * a.ndim)
    est = 2 * n_pages * page_rows * 2 * G * D * 4 + 2 * sum(_nbytes(a.shape, a.dtype) for a in consts) \
        + 6 * G * npg * CMP_STRIDE * D * 4
    return pl.pallas_call(
        functools.partial(_sample_cmp_kernel, n_pages=n_pages, G=G, hg=hg, n_tok=n_tok,
                          n_top=min(SEL_TOP, nsel), past_len=past_len, nc=nc),
        grid_spec=pltpu.PrefetchScalarGridSpec(
            num_scalar_prefetch=1,
            grid=(B,),
            in_specs=_page_specs(n_pages, page_rows, 0, layer)
            + [pl.BlockSpec((1, G, R, D), lambda b, pt: (b, 0, 0, 0))] + [full(a) for a in consts],
            out_specs=[pl.BlockSpec((1, G, R, D), lambda b, pt: (b, 0, 0, 0)),
                       pl.BlockSpec((1, G, R, nb), lambda b, pt: (b, 0, 0, 0))],
            scratch_shapes=[pltpu.VMEM((G * npg * HB_PITCH, D), f32)]),
        out_shape=[jax.ShapeDtypeStruct((B, G, R, D), f32),
                   jax.ShapeDtypeStruct((B, G, R, nb), f32)],
        compiler_params=_cparams(("parallel",), est + G * npg * HB_PITCH * D * 4),
        name="sample_cmp",
    )(page_table, *([cache_kv] * n_pages), q_s, *consts)


def _sample_attn_kernel(pt_ref, *refs, n_pages, G, hg, n_tok, past_len):
    D = LANES
    PR = refs[0].shape[0]
    pages2 = [_rows2d(r) for r in refs[:n_pages]]
    (q_ref, gate_ref, knew_ref, vnew_ref, kwnew_ref, vwnew_ref, wnew_ref, win_ref, sel_ref, oc_ref, kblk_ref,
     o_ref, nwin_ref) = refs[n_pages:]
    pitch = 2 * G
    R = hg * n_tok
    n_win = win_ref.shape[0]
    win2 = _rows2d(win_ref)
    c_exp = (D ** -0.5) * math.log2(math.e)
    t_row = past_len + lax.broadcasted_iota(jnp.int32, (R, 1), 0) % n_tok
    new_pos = past_len + lax.broadcasted_iota(jnp.int32, (1, PR), 1)
    new_ok = (new_pos <= t_row) & (new_pos < past_len + n_tok)
    wp_old = past_len - n_win + lax.broadcasted_iota(jnp.int32, (1, n_win), 1)
    vis_old = (wp_old <= t_row) & (wp_old > t_row - WINDOW) & (wp_old >= 0)
    vis_wnew = new_ok & (new_pos > t_row - WINDOW)
    zeros_new = jnp.zeros((PR - knew_ref.shape[2], D), bf16)
    gate = jax.nn.sigmoid(gate_ref[0])

    def softmax(s):
        e = jnp.exp2((s - jnp.max(s, axis=-1, keepdims=True)) * c_exp)
        return (e * (1.0 / jnp.maximum(jnp.sum(e, axis=-1, keepdims=True), 1e-30))).astype(bf16)

    for g in range(G):
        q = q_ref[0, g]
        q_aug = jnp.concatenate([q, (1.0 - sel_ref[0, g]).astype(bf16)], axis=1)
        s_parts = []
        for p in range(n_pages):
            kp = _slab(pages2[p], 0, g, PR, pitch).astype(bf16)
            s_parts.append(_dot_nt(q_aug, jnp.concatenate([kp, kblk_ref[p]], axis=1)))
        k_new = jnp.concatenate([knew_ref[0, g], zeros_new], axis=0)
        s_new = _dot_nt(q_aug, jnp.concatenate([k_new, kblk_ref[n_pages]], axis=1))
        s_parts.append(jnp.where(new_ok, s_new, NEG))
        p_s = softmax(jnp.concatenate(s_parts, axis=1))
        v_new = jnp.concatenate([vnew_ref[0, g], zeros_new], axis=0)
        o_s = jnp.dot(p_s[:, n_pages * PR:], v_new, preferred_element_type=f32)
        for p in range(n_pages):
            vp = _slab(pages2[p], 1, g, PR, pitch).astype(bf16)
            o_s = o_s + jnp.dot(p_s[:, p * PR:(p + 1) * PR], vp, preferred_element_type=f32)
        kw_old = _slab(win2, 0, g, n_win, pitch).astype(bf16)
        vw_old = _slab(win2, 1, g, n_win, pitch).astype(bf16)
        kw_new = jnp.concatenate([kwnew_ref[0, g], zeros_new], axis=0)
        vw_new = jnp.concatenate([vwnew_ref[0, g], zeros_new], axis=0)
        p_w = softmax(jnp.concatenate([jnp.where(vis_old, _dot_nt(q, kw_old), NEG),
                                       jnp.where(vis_wnew, _dot_nt(q, kw_new), NEG)], axis=1))
        o_w = (jnp.dot(p_w[:, :n_win], vw_old, preferred_element_type=f32)
               + jnp.dot(p_w[:, n_win:], vw_new, preferred_element_type=f32))
        gg = gate[g]
        o_ref[0, g] = (oc_ref[0, g] * gg[:, 0:1] + o_s * gg[:, 1:2] + o_w * gg[:, 2:3]).astype(o_ref.dtype)
    nwin_ref[0:n_win - n_tok] = win_ref[n_tok:n_win]
    nwin_ref[n_win - n_tok:n_win] = wnew_ref[...]


def _sample_attn(cache_kv, layer, page_table, q_s, gate_s, new_rows, win_new, state_win_kv, sel, o_c, n_tok):
    B, n_pages = page_table.shape
    page_rows = cache_kv.shape[2]
    G, D = N_KV_HEADS, LANES
    hg = N_Q_HEADS // G
    R = hg * n_tok
    past_len = n_pages * page_rows
    n_win = state_win_kv.shape[2]
    nb = sel.shape[-1]
    P8 = new_rows[0].shape[2]
    assert page_rows % SEL_BLOCK == 0 and nb == LANES and n_win >= n_tok
    kblk = jnp.stack([_key_block_onehot(page_rows, nb, (p * page_rows) // SEL_BLOCK) for p in range(n_pages + 1)])
    per_b = lambda shp: pl.BlockSpec((1,) + shp, lambda b, pt: (b,) + (0,) * len(shp))
    win_spec = pl.BlockSpec((None, None, n_win, 2, G, D), lambda b, pt: (layer, b, 0, 0, 0, 0))
    est = 2 * n_pages * page_rows * 2 * G * D * 4 + 4 * n_win * 2 * G * D * 4 + 40 * R * (past_len + page_rows) * 4
    return pl.pallas_call(
        functools.partial(_sample_attn_kernel, n_pages=n_pages, G=G, hg=hg, n_tok=n_tok, past_len=past_len),
        grid_spec=pltpu.PrefetchScalarGridSpec(
            num_scalar_prefetch=1,
            grid=(B,),
            in_specs=_page_specs(n_pages, page_rows, 1, layer)
            + [per_b((G, R, D)), per_b((G, R, LANES))] + [per_b((G, P8, D))] * 4
            + [pl.BlockSpec((None, n_tok, 2, G, D), lambda b, pt: (b, 0, 0, 0, 0)), win_spec,
               per_b((G, R, nb)), per_b((G, R, D)), pl.BlockSpec(kblk.shape, lambda b, pt: (0, 0, 0))],
            out_specs=[per_b((G, R, D)), win_spec]),
        out_shape=[jax.ShapeDtypeStruct((B, G, R, D), bf16),
                   jax.ShapeDtypeStruct(state_win_kv.shape, state_win_kv.dtype)],
        compiler_params=_cparams(("parallel",), est),
        name="sample_attn",
    )(page_table, *([cache_kv] * n_pages), q_s, gate_s, *new_rows, win_new, state_win_kv, sel, o_c, kblk)


def _pick_tile(n, pref):
    t = min(n, pref)
    while n % t:
        t //= 2
    return t


def _proj_layout(ret_w, qw, kvw):
    lay, off = {}, 0
    for name, w in (("rq", ret_w), ("rk", ret_w), ("rv", ret_w), ("rg", ret_w), ("nq", qw), ("nkv", 6 * kvw)):
        lay[name] = off
        off += w
    lay["width"] = off
    lay["ret_dk"] = ret_w // N_RET_HEADS
    return lay


def _prep_w_in(w_in, d_model, n_main):
    G = N_KV_HEADS
    hg = N_Q_HEADS // G
    w_t = jnp.swapaxes(w_in, 0, 1)
    n_gate = 3 * N_Q_HEADS
    ng = jnp.pad(w_t[n_main:n_main + n_gate].reshape(G, 3 * hg, d_model), ((0, 0), (0, LANES - 3 * hg), (0, 0)))
    return ((w_t, 0, n_main), (w_t, n_main + n_gate, 2 * d_model), (ng.reshape(G * LANES, d_model), 0, G * LANES))


def _in_proj(xn, w_in_parts, tm, tag):
    return [_mm([(xn, 0, xn.shape[1], w)], [], _ep_identity, f32, tm, _pick_tile(n, 512), n_out=n, b_row0=r0,
                name="in_proj_%s%d" % (tag, i)) for i, (w, r0, n) in enumerate(w_in_parts)]


def _dense_tail(x, y_ret, y_nsa, proj_g, wts, tm):
    w_ret_out, w_nsa_out, w_out, norm2_g, w_up, w_down = wts
    M, d_model = x.shape
    d_ff = w_up.shape[1]
    tn = _pick_tile(d_model, 512)
    m = _mm([(y_ret, 0, y_ret.shape[1], w_ret_out), (y_nsa, 0, y_nsa.shape[1], w_nsa_out)],
            [(proj_g, 0), (proj_g, d_model)], _ep_merge, bf16, tm, tn, name="merge")
    h = _mm([(m, 0, d_model, w_out)], [(x, 0)], _ep_residual, f32, tm, tn, name="out_proj")
    hn = _rmsnorm(h, norm2_g, _pick_tile(M, 256))
    u = _mm([(hn, 0, d_model, w_up)], [], _ep_relu2, bf16, tm, _pick_tile(d_ff, 512), name="mlp_up")
    return _mm([(u, 0, d_ff, w_down)], [(h, 0)], _ep_residual, f32, tm, _pick_tile(d_model, 1024),
               tk=_pick_tile(d_ff, 2048), name="mlp_down")


def kernel(x_prompt, x_sample, cache_kv, page_table, state_win_kv, state_ret, norm1_g, w_in, w_ret_out,
           q_norm_g, kc_norm_g, ks_norm_g, kw_norm_g, cmp_pe_k, cmp_w1_k, cmp_b1_k, cmp_w2_k,
           cmp_pe_v, cmp_w1_v, cmp_b1_v, cmp_w2_v, w_nsa_out, w_out, norm2_g, w_up, w_down):
    depth = w_in.shape[0]
    assert depth == 1 and x_prompt.shape[0] == 1
    B, T, d_model = x_prompt.shape
    Bs, Ts, _ = x_sample.shape
    G, D = N_KV_HEADS, LANES
    hg = N_Q_HEADS // G
    qw, kvw = N_Q_HEADS * D, G * D
    ret_w = w_ret_out.shape[1]
    dk = ret_w // N_RET_HEADS
    page_rows = cache_kv.shape[2]
    n_pages = page_table.shape[1]
    past_len = n_pages * page_rows
    lay = _proj_layout(ret_w, qw, kvw)

    l = 0
    w_in_parts = _prep_w_in(w_in[l], d_model, lay["width"])
    wts = (w_ret_out[l], w_nsa_out[l], w_out[l], norm2_g[l], w_up[l], w_down[l])
    cwk = _cmp_weights(cmp_pe_k[l], cmp_w1_k[l], cmp_b1_k[l], cmp_w2_k[l])
    cwv = _cmp_weights(cmp_pe_v[l], cmp_w1_v[l], cmp_b1_v[l], cmp_w2_v[l])

    xp = x_prompt.reshape(T, d_model)
    tm = _pick_tile(T, 1024)
    xn = _rmsnorm(xp, norm1_g[l], _pick_tile(T, 256))
    proj, proj_g, gates = _in_proj(xn, w_in_parts, tm, "p")
    qn, paged, win, kvb = _nsa_prep(proj, lay, q_norm_g[l], ks_norm_g[l], kw_norm_g[l], _pick_tile(T, 256))
    nhb = T // CMP_STRIDE
    half_blocks = (paged[:, :2 * kvw].astype(bf16).reshape(nhb, CMP_STRIDE, 2, G, D)
                   .transpose(2, 3, 0, 1, 4).reshape(2, G, nhb, CMP_STRIDE * D))
    kc = _compress_prompt(half_blocks[0], cwk, kc_norm_g[l], True)
    vc = _compress_prompt(half_blocks[1], cwv, kc_norm_g[l], False)
    y_nsa = _nsa_prompt(qn, gates, kc, vc, kvb)
    y_ret, s_prompt = _ret_prompt(proj, lay, T)
    y_prompt = _dense_tail(xp, y_ret, y_nsa, proj_g, wts, tm).reshape(B, T, d_model)
    new_kv_prompt = paged.reshape(1, B, T, 4, G, D)
    n_keep = min(WINDOW, T)
    new_win_prompt = win[T - n_keep:].reshape(1, B, n_keep, 2, G, D)
    new_ret_prompt = s_prompt.reshape(1, B, N_RET_HEADS, dk, dk)

    Ms = Bs * Ts
    xs = x_sample.reshape(Ms, d_model)
    tms = _pick_tile(Ms, 512)
    xn = _rmsnorm(xs, norm1_g[l], _pick_tile(Ms, 256))
    proj_s, proj_gs, gates = _in_proj(xn, w_in_parts, tms, "s")
    qn, paged_s, win_s, kvb = _nsa_prep(proj_s, lay, q_norm_g[l], ks_norm_g[l], kw_norm_g[l], _pick_tile(Ms, 256))
    P8 = 8
    assert Ts <= P8
    proj8 = jnp.pad(proj_s.reshape(Bs, Ts, lay["width"]), ((0, 0), (0, P8 - Ts), (0, 0)))
    y_ret8, s_sample = _ret_sample(proj8, lay, state_ret[l], past_len, Ts)
    y_ret = y_ret8[:, :Ts].reshape(Ms, ret_w)
    q_s = qn.reshape(Bs, Ts, G, hg, D).transpose(0, 2, 3, 1, 4).reshape(Bs, G, hg * Ts, D)
    gate_s = (gates.reshape(Bs, Ts, G, LANES)[..., :3 * hg]
              .reshape(Bs, Ts, G, hg, 3).transpose(0, 2, 3, 1, 4).reshape(Bs, G, hg * Ts, 3))
    gate_s = jnp.pad(gate_s, ((0, 0), (0, 0), (0, 0), (0, LANES - 3)))
    kvb5 = jnp.pad(kvb.reshape(Bs, Ts, 4, G, D), ((0, 0), (0, P8 - Ts), (0, 0), (0, 0), (0, 0)))
    new_rows = [kvb5[:, :, c].transpose(0, 2, 1, 3) for c in range(4)]
    win_new = win_s.reshape(Bs, Ts, 2, G, D)
    o_c, sel = _sample_cmp(cache_kv, l, page_table, q_s, cwk, cwv, kc_norm_g[l], Ts)
    o_s, new_win_sample = _sample_attn(cache_kv, l, page_table, q_s, gate_s, new_rows, win_new, state_win_kv,
                                       sel, o_c, Ts)
    y_nsa = o_s.reshape(Bs, G, hg, Ts, D).transpose(0, 3, 1, 2, 4).reshape(Ms, qw)
    y_sample = _dense_tail(xs, y_ret, y_nsa, proj_gs, wts, tms).reshape(Bs, Ts, d_model)
    new_kv_sample = paged_s.reshape(1, Bs, Ts, 4, G, D)
    new_ret_sample = s_sample.reshape(1, Bs, N_RET_HEADS, dk, dk)
    return (y_prompt, y_sample, new_kv_prompt, new_win_prompt, new_ret_prompt,
            new_kv_sample, new_win_sample, new_ret_sample)
```

```python
import functools
import math

import numpy as np
import jax
import jax.numpy as jnp
from jax import lax
from jax.experimental import pallas as pl
from jax.experimental.pallas import tpu as pltpu

N_RET_HEADS = 8
RET_CHUNK = 128
ROPE_BASE = 10000.0
N_Q_HEADS = 16
N_KV_HEADS = 4
CMP_LEN = 32
CMP_STRIDE = 16
SEL_BLOCK = 64
SEL_TOP = 16
WINDOW = 512
Q_BLOCK = 256
EPS = 1e-6

LANES = 128
VMEM_BYTES = 64 * 1024 * 1024
NEG = -1e30
SEL_TILE = 1024

f32 = jnp.float32
bf16 = jnp.bfloat16


def _cparams(sem, est_bytes):
    limit = int(min(VMEM_BYTES - (4 << 20), max(est_bytes + (8 << 20), 32 << 20)))
    return pltpu.CompilerParams(dimension_semantics=sem, vmem_limit_bytes=limit)


def _nbytes(shape, dtype):
    return int(np.prod(shape)) * jnp.dtype(dtype).itemsize


def _rmsnorm_kernel(x_ref, g_ref, o_ref):
    x = x_ref[...]
    y = x * lax.rsqrt(jnp.mean(x * x, axis=-1, keepdims=True) + EPS)
    o_ref[...] = (y * g_ref[...]).astype(o_ref.dtype)


def _rmsnorm(x, g, tm):
    M, D = x.shape
    return pl.pallas_call(
        _rmsnorm_kernel,
        grid=(M // tm,),
        in_specs=[pl.BlockSpec((tm, D), lambda i: (i, 0)),
                  pl.BlockSpec((1, D), lambda i: (0, 0))],
        out_specs=pl.BlockSpec((tm, D), lambda i: (i, 0)),
        out_shape=jax.ShapeDtypeStruct((M, D), bf16),
        compiler_params=_cparams(("parallel",), 2 * tm * D * 6),
        name="rmsnorm",
    )(x, g.reshape(1, D))


def _mm_kernel(*refs, n_pairs, n_extra, nk, epilogue, b_rows):
    a = refs[:n_pairs]
    b = refs[n_pairs:2 * n_pairs]
    ex = refs[2 * n_pairs:2 * n_pairs + n_extra]
    o_ref = refs[2 * n_pairs + n_extra]
    if nk == 1:
        dots = [(_dot_nt if b_rows[i] else functools.partial(jnp.dot, preferred_element_type=f32))(
            a[i][...], b[i][...].astype(bf16)) for i in range(n_pairs)]
        o_ref[...] = epilogue(dots, [e[...] for e in ex]).astype(o_ref.dtype)
    else:
        acc_ref = refs[-1]
        k = pl.program_id(2)

        @pl.when(k == 0)
        def _():
            acc_ref[...] = jnp.zeros_like(acc_ref)

        acc_ref[...] += jnp.dot(a[0][...], b[0][...].astype(bf16), preferred_element_type=f32)

        @pl.when(k == nk - 1)
        def _():
            o_ref[...] = epilogue([acc_ref[...]], [e[...] for e in ex]).astype(o_ref.dtype)


def _mm(pairs, extras, epilogue, out_dtype, tm, tn, tk=None, n_out=None, b_row0=None, name="mm"):
    M = pairs[0][0].shape[0]
    N = pairs[0][3].shape[1] if n_out is None else n_out
    K0 = pairs[0][2]
    tk = K0 if tk is None else tk
    nk = K0 // tk
    assert M % tm == 0 and N % tn == 0 and K0 % tk == 0
    assert nk == 1 or (len(pairs) == 1 and b_row0 is None)
    in_specs, args, est = [], [], 0
    for (a, off, K, b) in pairs:
        kb = K if nk == 1 else tk
        assert off % kb == 0
        in_specs.append(pl.BlockSpec((tm, kb), functools.partial(lambda i, j, k, o: (i, o + k), o=off // kb)))
        args.append(a)
        est += 2 * _nbytes((tm, kb), a.dtype)
    for (a, off, K, b) in pairs:
        kb = K if nk == 1 else tk
        if b_row0 is None:
            in_specs.append(pl.BlockSpec((kb, tn), lambda i, j, k: (k, j)))
        else:
            assert b_row0 % 8 == 0 and tn % 8 == 0
            in_specs.append(pl.BlockSpec((pl.Element(tn), pl.Element(kb)),
                                         lambda i, j, k: (pl.multiple_of(b_row0 + j * tn, 8), 0)))
        args.append(b)
        est += 2 * _nbytes((kb, tn), b.dtype) + (0 if b.dtype == bf16 else _nbytes((kb, tn), bf16))
    for (e, off) in extras:
        assert off % tn == 0
        in_specs.append(pl.BlockSpec((tm, tn), functools.partial(lambda i, j, k, o: (i, o + j), o=off // tn)))
        args.append(e)
        est += 2 * _nbytes((tm, tn), e.dtype)
    est += 2 * _nbytes((tm, tn), out_dtype) + (len(pairs) + 2) * tm * tn * 4
    scratch = [pltpu.VMEM((tm, tn), f32)] if nk > 1 else []
    return pl.pallas_call(
        functools.partial(_mm_kernel, n_pairs=len(pairs), n_extra=len(extras), nk=nk, epilogue=epilogue,
                          b_rows=[b_row0 is not None] * len(pairs)),
        grid=(M // tm, N // tn, nk),
        in_specs=in_specs,
        out_specs=pl.BlockSpec((tm, tn), lambda i, j, k: (i, j)),
        out_shape=jax.ShapeDtypeStruct((M, N), out_dtype),
        scratch_shapes=scratch,
        compiler_params=_cparams(("parallel", "parallel", "arbitrary"), est),
        name=name,
    )(*args)


def _ep_identity(dots, ex):
    return dots[0]


def _ep_merge(dots, ex):
    return jax.nn.sigmoid(ex[0]) * dots[0] + jax.nn.sigmoid(ex[1]) * dots[1]


def _ep_residual(dots, ex):
    return ex[0] + dots[0]


def _ep_relu2(dots, ex):
    return jnp.square(jnp.maximum(dots[0], 0.0))


def _head_norm(x, g):
    return x * lax.rsqrt(jnp.mean(x * x, axis=-1, keepdims=True) + EPS) * g


def _nsa_prep_kernel(q_ref, kv01_ref, kv23_ref, kv45_ref, qg_ref, ksg_ref, kwg_ref,
                     qn_ref, paged_ref, win_ref, kvb_ref, *, n_q, n_kv):
    D = LANES
    kvw = n_kv * D
    for h in range(n_q):
        sl = slice(h * D, (h + 1) * D)
        qn_ref[:, sl] = _head_norm(q_ref[:, sl], qg_ref[...]).astype(qn_ref.dtype)
    paged_ref[:, 0:2 * kvw] = kv01_ref[...]
    for g in range(n_kv):
        sl = slice(g * D, (g + 1) * D)
        ks = _head_norm(kv23_ref[:, sl], ksg_ref[...])
        paged_ref[:, 2 * kvw + g * D:2 * kvw + (g + 1) * D] = ks
        kvb_ref[:, sl] = ks.astype(kvb_ref.dtype)
        kw = _head_norm(kv45_ref[:, sl], kwg_ref[...])
        win_ref[:, sl] = kw
        kvb_ref[:, 2 * kvw + g * D:2 * kvw + (g + 1) * D] = kw.astype(kvb_ref.dtype)
    vs = kv23_ref[:, kvw:2 * kvw]
    paged_ref[:, 3 * kvw:4 * kvw] = vs
    kvb_ref[:, kvw:2 * kvw] = vs.astype(kvb_ref.dtype)
    vw = kv45_ref[:, kvw:2 * kvw]
    win_ref[:, kvw:2 * kvw] = vw
    kvb_ref[:, 3 * kvw:4 * kvw] = vw.astype(kvb_ref.dtype)


def _nsa_prep(proj, lay, q_norm_g, ks_norm_g, kw_norm_g, tm):
    M = proj.shape[0]
    n_q, n_kv, D = N_Q_HEADS, N_KV_HEADS, LANES
    qw, kvw = n_q * D, n_kv * D
    assert lay["nq"] % qw == 0 and lay["nkv"] % (2 * kvw) == 0
    kvb0 = lay["nkv"] // (2 * kvw)
    gspec = pl.BlockSpec((1, D), lambda i: (0, 0))
    return pl.pallas_call(
        functools.partial(_nsa_prep_kernel, n_q=n_q, n_kv=n_kv),
        grid=(M // tm,),
        in_specs=[pl.BlockSpec((tm, qw), functools.partial(lambda i, o: (i, o), o=lay["nq"] // qw)),
                  pl.BlockSpec((tm, 2 * kvw), functools.partial(lambda i, o: (i, o), o=kvb0)),
                  pl.BlockSpec((tm, 2 * kvw), functools.partial(lambda i, o: (i, o), o=kvb0 + 1)),
                  pl.BlockSpec((tm, 2 * kvw), functools.partial(lambda i, o: (i, o), o=kvb0 + 2)),
                  gspec, gspec, gspec],
        out_specs=[pl.BlockSpec((tm, qw), lambda i: (i, 0)),
                   pl.BlockSpec((tm, 4 * kvw), lambda i: (i, 0)),
                   pl.BlockSpec((tm, 2 * kvw), lambda i: (i, 0)),
                   pl.BlockSpec((tm, 4 * kvw), lambda i: (i, 0))],
        out_shape=[jax.ShapeDtypeStruct((M, qw), bf16),
                   jax.ShapeDtypeStruct((M, 4 * kvw), f32),
                   jax.ShapeDtypeStruct((M, 2 * kvw), f32),
                   jax.ShapeDtypeStruct((M, 4 * kvw), bf16)],
        compiler_params=_cparams(("parallel",), 2 * tm * (qw * 6 + kvw * 6 * 4 + kvw * 6 * 4 + kvw * 8)),
        name="nsa_prep",
    )(proj, proj, proj, proj, q_norm_g.reshape(1, D), ks_norm_g.reshape(1, D), kw_norm_g.reshape(1, D))


def _gelu_tanh(x):
    return 0.5 * x * (1.0 + jnp.tanh(math.sqrt(2.0 / math.pi) * (x + 0.044715 * (x * x * x))))


def _compress_rows(H, w1t, w1b, pe, b1, w2, step=1):
    half = w1t.shape[0]
    P = jnp.dot(H, w1t, preferred_element_type=f32)
    Q = jnp.dot(H, w1b, preferred_element_type=f32)
    bias = (jnp.dot(pe[:, :half], w1t, preferred_element_type=f32)
            + jnp.dot(pe[:, half:], w1b, preferred_element_type=f32))[0:1] + b1
    hidden = P + pltpu.roll(Q, Q.shape[0] - step, 0) + bias
    return jnp.dot(_gelu_tanh(hidden).astype(bf16), w2, preferred_element_type=f32)


def _compress_kernel(h_ref, w1t_ref, w1b_ref, pe_ref, b1_ref, w2_ref, g_ref, o_ref, *, normalize):
    out = _compress_rows(h_ref[0], w1t_ref[...], w1b_ref[...], pe_ref[...], b1_ref[...], w2_ref[...])
    if normalize:
        out = _head_norm(out, g_ref[...])
    o_ref[0] = out.astype(o_ref.dtype)


def _cmp_weights(pe, w1, b1, w2):
    half = w1.shape[0] // 2
    pe8 = jnp.broadcast_to(pe.reshape(1, -1), (8, pe.size)).astype(bf16)
    return (w1[:half].astype(bf16), w1[half:].astype(bf16), pe8, b1.reshape(1, -1).astype(f32), w2.astype(bf16))


def _compress_prompt(H, cwk, g, normalize):
    G, n, W = H.shape
    w1t, w1b, pe8, b1, w2 = cwk
    hid, D = w2.shape
    full = lambda a: pl.BlockSpec(a.shape, lambda i: (0,) * a.ndim)
    g2 = g.reshape(1, D)
    return pl.pallas_call(
        functools.partial(_compress_kernel, normalize=normalize),
        grid=(G,),
        in_specs=[pl.BlockSpec((1, n, W), lambda i: (i, 0, 0)), full(w1t), full(w1b), full(pe8), full(b1), full(w2),
                  full(g2)],
        out_specs=pl.BlockSpec((1, n, D), lambda i: (i, 0, 0)),
        out_shape=jax.ShapeDtypeStruct((G, n, D), bf16),
        compiler_params=_cparams(("parallel",), 2 * (n * W * 2 + 2 * W * hid * 2) + 6 * n * hid * 4),
        name="compress_prompt",
    )(H, w1t, w1b, pe8, b1, w2, g2)


def _dot_nt(a, b):
    return lax.dot_general(a, b, (((1,), (1,)), ((), ())), preferred_element_type=f32)


def _dot_split(a, b):
    hi = a.astype(bf16)
    r1 = a - hi.astype(f32)
    mid = r1.astype(bf16)
    lo = (r1 - mid.astype(f32)).astype(bf16)
    return (jnp.dot(hi, b, preferred_element_type=f32) + jnp.dot(mid, b, preferred_element_type=f32)
            + jnp.dot(lo, b, preferred_element_type=f32))


def _softmax_masked(s, maskf):
    vis = maskf > 0.0
    sm = jnp.where(vis, s, NEG)
    m = jnp.max(sm, axis=-1, keepdims=True)
    e = jnp.where(vis, jnp.exp(sm - m), 0.0)
    return e / jnp.maximum(jnp.sum(e, axis=-1, keepdims=True), 1e-30)


def _select_blocks(imp, cur, n_top, axis):
    nb = imp.shape[axis]
    blk = lax.broadcasted_iota(jnp.int32, imp.shape, axis).astype(f32)
    forced = jnp.where(blk == 0.0, 1.0, 0.0) + jnp.where(blk == cur, 1.0, 0.0) + jnp.where(blk == cur - 1.0, 1.0, 0.0)
    score = jnp.where(forced > 0.0, jnp.inf, jnp.where(blk <= cur, imp, -jnp.inf))
    sel = jnp.zeros(imp.shape, f32)
    for _ in range(n_top):
        mx = jnp.max(score, axis=axis, keepdims=True)
        first = jnp.min(jnp.where(score == mx, blk, float(nb)), axis=axis, keepdims=True)
        hit = blk == first
        sel = jnp.where(hit, 1.0, sel)
        score = jnp.where(hit, -jnp.inf, score)
    return sel


def _nsa_prompt_kernel(q_ref, gate_ref, kc_ref, vc_ref, ks_ref, vs_ref, kw_ref, vw_ref, c2s_ref, kblk_ref, wbias_ref,
                       o_ref, *, hg, n_top, tk):
    D = LANES
    QB = q_ref.shape[0]
    T = ks_ref.shape[0]
    nc = kc_ref.shape[1]
    R = hg * QB
    c_exp = (D ** -0.5) * math.log2(math.e)
    qb = pl.program_id(1)
    s0 = qb * QB
    q = jnp.concatenate([q_ref[:, h * D:(h + 1) * D] for h in range(hg)], axis=0)
    t_row = s0 + (lax.broadcasted_iota(jnp.int32, (R, 1), 0) & (QB - 1))

    kc_end = lax.broadcasted_iota(jnp.int32, (1, nc), 1) * CMP_STRIDE + (CMP_LEN - 1)
    s = jnp.where(kc_end <= t_row, _dot_nt(q, kc_ref[0]), NEG)
    e = jnp.exp2((s - jnp.max(s, axis=-1, keepdims=True)) * c_exp)
    row_ok = jnp.where(t_row >= CMP_LEN - 1, 1.0, 0.0)
    inv = row_ok / jnp.maximum(jnp.sum(e, axis=-1, keepdims=True), 1e-30)
    o_c = jnp.dot(e.astype(bf16), vc_ref[0], preferred_element_type=f32) * inv

    wl = wbias_ref.shape[2]
    w0 = pl.multiple_of(jnp.clip(s0 - WINDOW, 0, T - wl), QB)
    bias = wbias_ref[jnp.minimum(qb, wbias_ref.shape[0] - 1)]
    sw = _dot_nt(q, kw_ref[pl.ds(w0, wl), :])
    sw = jnp.concatenate([sw[h * QB:(h + 1) * QB] + bias for h in range(hg)], axis=0)
    ew = jnp.exp2((sw - jnp.max(sw, axis=-1, keepdims=True)) * c_exp)
    o_w = (jnp.dot(ew.astype(bf16), vw_ref[pl.ds(w0, wl), :], preferred_element_type=f32)
           * (1.0 / jnp.maximum(jnp.sum(ew, axis=-1, keepdims=True), 1e-30)))

    p_sum = e[0:QB] * inv[0:QB]
    for h in range(1, hg):
        p_sum = p_sum + e[h * QB:(h + 1) * QB] * inv[h * QB:(h + 1) * QB]
    imp_t = _dot_split(p_sum, c2s_ref[...]).T
    cur_t = ((s0 + lax.broadcasted_iota(jnp.int32, (1, QB), 1)) // SEL_BLOCK).astype(f32)
    not_sel = (1.0 - _select_blocks(imp_t, cur_t, n_top, 0)).T.astype(bf16)

    q_aug = jnp.concatenate([q, jnp.concatenate([not_sel] * hg, axis=0)], axis=1)
    col_pos = lax.broadcasted_iota(jnp.int32, (1, tk), 1)

    def sel_tile(kt, carry, causal):
        m, l, acc = carry
        k0 = pl.multiple_of(kt * tk, tk)
        s = _dot_nt(q_aug, jnp.concatenate([ks_ref[pl.ds(k0, tk), :], kblk_ref[pl.ds(k0, tk), :]], axis=1))
        if causal:
            s = jnp.where(col_pos + k0 <= t_row, s, NEG)
        m_new = jnp.maximum(m, jnp.max(s, axis=-1, keepdims=True))
        alpha = jnp.exp2((m - m_new) * c_exp)
        e = jnp.exp2((s - m_new) * c_exp)
        l = alpha * l + jnp.sum(e, axis=-1, keepdims=True)
        acc = alpha * acc + jnp.dot(e.astype(bf16), vs_ref[pl.ds(k0, tk), :], preferred_element_type=f32)
        return m_new, l, acc

    last = (s0 + QB - 1) // tk
    init = (jnp.full((R, 1), NEG, f32), jnp.zeros((R, 1), f32), jnp.zeros((R, D), f32))
    carry = lax.fori_loop(0, last, functools.partial(sel_tile, causal=False), init)
    _, l, acc = sel_tile(last, carry, True)
    o_s = acc * (1.0 / jnp.maximum(l, 1e-30))

    gate = jax.nn.sigmoid(gate_ref[...])
    for h in range(hg):
        rows = slice(h * QB, (h + 1) * QB)
        o = (o_c[rows] * gate[:, 3 * h:3 * h + 1] + o_s[rows] * gate[:, 3 * h + 1:3 * h + 2]
             + o_w[rows] * gate[:, 3 * h + 2:3 * h + 3])
        o_ref[:, h * D:(h + 1) * D] = o.astype(o_ref.dtype)


def _cmp_to_sel(nc_pad, nsel_pad, nc, nsel):
    cs = np.arange(nc_pad)[:, None] * CMP_STRIDE
    bs = np.arange(nsel_pad)[None, :] * SEL_BLOCK
    ov = np.clip(np.minimum(cs + CMP_LEN, bs + SEL_BLOCK) - np.maximum(cs, bs), 0, None).astype(np.float32) / CMP_LEN
    ov[nc:, :] = 0.0
    ov[:, nsel:] = 0.0
    return jnp.asarray(ov, dtype=bf16)


def _key_block_onehot(n_keys, nb, first_block=0):
    blk = first_block + np.arange(n_keys)[:, None] // SEL_BLOCK
    return jnp.asarray(np.where(blk == np.arange(nb)[None, :], NEG, 0.0), dtype=bf16)


def _window_bias(QB, wl, n_off):
    d = (np.arange(n_off)[:, None, None] * QB + np.arange(QB)[None, :, None] - np.arange(wl)[None, None, :])
    return jnp.asarray(np.where((d >= 0) & (d < WINDOW), 0.0, NEG), dtype=f32)


def _nsa_prompt(qn, gates, kc, vc, kvb):
    T = qn.shape[0]
    G, D = N_KV_HEADS, LANES
    hg = N_Q_HEADS // G
    QB = Q_BLOCK
    nc_pad = kc.shape[1]
    nc = (T - CMP_LEN) // CMP_STRIDE + 1
    nsel = -(-T // SEL_BLOCK)
    nb = -(-nsel // LANES) * LANES
    c2s = _cmp_to_sel(nc_pad, nb, nc, nsel)
    kblk = _key_block_onehot(T, nb)
    tk = min(SEL_TILE, T)
    wl = WINDOW + QB
    assert T % tk == 0 and T % QB == 0 and WINDOW % QB == 0 and T >= wl and tk % QB == 0
    wbias = _window_bias(QB, wl, WINDOW // QB + 1)
    est = 2 * (4 * T * D * 2 + T * nb * 2 + wbias.size * 4) + 12 * hg * QB * max(tk, nc_pad, wl) * 4
    return pl.pallas_call(
        functools.partial(_nsa_prompt_kernel, hg=hg, n_top=min(SEL_TOP, nsel), tk=tk),
        grid=(G, T // QB),
        in_specs=[pl.BlockSpec((QB, hg * D), lambda g, i: (i, g)),
                  pl.BlockSpec((QB, LANES), lambda g, i: (i, g)),
                  pl.BlockSpec((1, nc_pad, D), lambda g, i: (g, 0, 0)),
                  pl.BlockSpec((1, nc_pad, D), lambda g, i: (g, 0, 0)),
                  pl.BlockSpec((T, D), lambda g, i: (0, g)),
                  pl.BlockSpec((T, D), functools.partial(lambda g, i, o: (0, o + g), o=G)),
                  pl.BlockSpec((T, D), functools.partial(lambda g, i, o: (0, o + g), o=2 * G)),
                  pl.BlockSpec((T, D), functools.partial(lambda g, i, o: (0, o + g), o=3 * G)),
                  pl.BlockSpec((nc_pad, nb), lambda g, i: (0, 0)),
                  pl.BlockSpec((T, nb), lambda g, i: (0, 0)),
                  pl.BlockSpec(wbias.shape, lambda g, i: (0, 0, 0))],
        out_specs=pl.BlockSpec((QB, hg * D), lambda g, i: (i, g)),
        out_shape=jax.ShapeDtypeStruct((T, N_Q_HEADS * D), bf16),
        compiler_params=_cparams(("parallel", "parallel"), est),
        name="nsa_prompt",
    )(qn, gates, kc, vc, kvb, kvb, kvb, kvb, c2s, kblk, wbias)


def _ret_tables(C, n_valid):
    lg = np.log1p(-np.exp2(-5.0 - np.arange(N_RET_HEADS, dtype=np.float64)))
    i = np.arange(C, dtype=np.float64)
    diff = i[:, None] - i[None, :]
    valid = (i < n_valid)
    dmask = np.where(diff >= 0, np.exp(lg[:, None, None] * np.maximum(diff, 0.0)), 0.0)
    dmask = dmask * (valid[:, None] & valid[None, :])
    q_dec = np.exp(lg[:, None] * (i[None, :] + 1.0)) * valid[None, :]
    k_dec = np.exp(lg[:, None] * (n_valid - 1.0 - i[None, :])) * valid[None, :]
    c_dec = np.exp(lg * n_valid)
    as32 = lambda a: jnp.asarray(a, dtype=f32)
    return (as32(dmask), as32(q_dec[:, :, None]), as32(k_dec[:, :, None]),
            as32(np.broadcast_to(c_dec[:, None, None], (N_RET_HEADS, 1, LANES))))


def _rope_tables(pos, half):
    inv = jnp.power(ROPE_BASE, -jnp.arange(half, dtype=f32) / half)
    ang = pos.astype(f32)[:, None] * inv[None, :]
    return jnp.cos(ang), jnp.sin(ang)


def _rotary(x, cos, sin):
    half = cos.shape[-1]
    x1, x2 = x[:, :half], x[:, half:]
    return jnp.concatenate([x1 * cos - x2 * sin, x1 * sin + x2 * cos], axis=-1)


def _ret_chunk(q_raw, k_raw, v, rg, cos, sin, dmask, q_dec, k_dec, c_dec, S):
    dk = q_raw.shape[-1]
    q = _rotary(q_raw, cos, sin)
    k = _rotary(k_raw, cos, sin) * (dk ** -0.5)
    vb = v.astype(bf16)
    inner = _dot_nt(q.astype(bf16), k.astype(bf16)) * dmask
    o = (jnp.dot(inner.astype(bf16), vb, preferred_element_type=f32)
         + jnp.dot((q * q_dec).astype(bf16), S.astype(bf16), preferred_element_type=f32))
    S_new = S * c_dec + lax.dot_general((k * k_dec).astype(bf16), vb, (((0,), (0,)), ((), ())),
                                        preferred_element_type=f32)
    y = o * lax.rsqrt(jnp.mean(o * o, axis=-1, keepdims=True) + EPS)
    return y * (rg * jax.nn.sigmoid(rg)), S_new


def _ret_prompt_kernel(q_ref, k_ref, v_ref, g_ref, cos_ref, sin_ref, dm_ref, qd_ref, kd_ref, cd_ref,
                       y_ref, s_out_ref, s_ref, *, hp, dk):
    c = pl.program_id(1)

    @pl.when(c == 0)
    def _():
        s_ref[...] = jnp.zeros_like(s_ref)

    for h in range(hp):
        sl = slice(h * dk, (h + 1) * dk)
        y, S_new = _ret_chunk(q_ref[:, sl], k_ref[:, sl], v_ref[:, sl], g_ref[:, sl], cos_ref[...], sin_ref[...],
                              dm_ref[h], qd_ref[h], kd_ref[h], cd_ref[h][:, 0:1], s_ref[h])
        y_ref[:, sl] = y.astype(y_ref.dtype)
        s_ref[h] = S_new

    @pl.when(c == pl.num_programs(1) - 1)
    def _():
        s_out_ref[...] = s_ref[...]


def _ret_prompt(proj, lay, T):
    H = N_RET_HEADS
    dk = lay["ret_dk"]
    C = min(RET_CHUNK, T)
    hp = 2 if H % 2 == 0 else 1
    w = hp * dk
    assert T % C == 0 and all(lay[n] % w == 0 for n in ("rq", "rk", "rv", "rg"))
    cos, sin = _rope_tables(jnp.arange(T), dk // 2)
    dmask, q_dec, k_dec, c_dec = _ret_tables(C, C)
    col = lambda name: functools.partial(lambda h, c, o: (c, o + h), o=lay[name] // w)
    tab = lambda shp: pl.BlockSpec((hp,) + shp, lambda h, c: (h, 0, 0))
    return pl.pallas_call(
        functools.partial(_ret_prompt_kernel, hp=hp, dk=dk),
        grid=(H // hp, T // C),
        in_specs=[pl.BlockSpec((C, w), col("rq")), pl.BlockSpec((C, w), col("rk")),
                  pl.BlockSpec((C, w), col("rv")), pl.BlockSpec((C, w), col("rg")),
                  pl.BlockSpec((C, dk // 2), lambda h, c: (c, 0)), pl.BlockSpec((C, dk // 2), lambda h, c: (c, 0)),
                  tab((C, C)), tab((C, 1)), tab((C, 1)), tab((1, LANES))],
        out_specs=[pl.BlockSpec((C, w), lambda h, c: (c, h)),
                   pl.BlockSpec((hp, dk, dk), lambda h, c: (h, 0, 0))],
        out_shape=[jax.ShapeDtypeStruct((T, H * dk), bf16),
                   jax.ShapeDtypeStruct((H, dk, dk), f32)],
        scratch_shapes=[pltpu.VMEM((hp, dk, dk), f32)],
        compiler_params=_cparams(("parallel", "arbitrary"), hp * (16 * C * dk * 4 + 4 * dk * dk * 4)),
        name="ret_prompt",
    )(proj, proj, proj, proj, cos, sin, dmask, q_dec, k_dec, c_dec)


def _ret_sample_kernel(q_ref, k_ref, v_ref, g_ref, cos_ref, sin_ref, dm_ref, qd_ref, kd_ref, cd_ref, s0_ref,
                       y_ref, s_out_ref, *, n_heads, dk):
    for h in range(n_heads):
        sl = slice(h * dk, (h + 1) * dk)
        y, S_new = _ret_chunk(q_ref[0][:, sl], k_ref[0][:, sl], v_ref[0][:, sl], g_ref[0][:, sl],
                              cos_ref[...], sin_ref[...], dm_ref[h], qd_ref[h], kd_ref[h], cd_ref[h][:, 0:1],
                              s0_ref[0, h])
        y_ref[0, :, sl] = y.astype(y_ref.dtype)
        s_out_ref[0, h] = S_new


def _ret_sample(proj8, lay, state, past_len, n_tok):
    B, P, _ = proj8.shape
    H = N_RET_HEADS
    dk = lay["ret_dk"]
    w = H * dk
    assert all(lay[n] % w == 0 for n in ("rq", "rk", "rv", "rg"))
    cos, sin = _rope_tables(past_len + jnp.arange(P), dk // 2)
    dmask, q_dec, k_dec, c_dec = _ret_tables(P, n_tok)
    col = lambda name: functools.partial(lambda b, o: (b, 0, o), o=lay[name] // w)
    full = lambda a: pl.BlockSpec(a.shape, lambda b: (0,) * a.ndim)
    return pl.pallas_call(
        functools.partial(_ret_sample_kernel, n_heads=H, dk=dk),
        grid=(B,),
        in_specs=[pl.BlockSpec((1, P, w), col("rq")), pl.BlockSpec((1, P, w), col("rk")),
                  pl.BlockSpec((1, P, w), col("rv")), pl.BlockSpec((1, P, w), col("rg")),
                  full(cos), full(sin), full(dmask), full(q_dec), full(k_dec), full(c_dec),
                  pl.BlockSpec((1, H, dk, dk), lambda b: (b, 0, 0, 0))],
        out_specs=[pl.BlockSpec((1, P, w), lambda b: (b, 0, 0)),
                   pl.BlockSpec((1, H, dk, dk), lambda b: (b, 0, 0, 0))],
        out_shape=[jax.ShapeDtypeStruct((B, P, w), bf16),
                   jax.ShapeDtypeStruct((B, H, dk, dk), f32)],
        compiler_params=_cparams(("parallel",), 4 * H * dk * dk * 4 + 16 * P * w * 4),
        name="ret_sample",
    )(proj8, proj8, proj8, proj8, cos, sin, dmask, q_dec, k_dec, c_dec, state)


def _rows2d(r):
    return r.reshape(r.shape[0] * r.shape[1] * r.shape[2], r.shape[3])


def _slab(r2, kind, g, n_rows, pitch):
    return r2[pl.ds(kind * (pitch // 2) + g, n_rows, stride=pitch), :]


def _page_half_blocks(page_refs, kind):
    n_hb = page_refs[0].shape[0] // CMP_STRIDE
    cols = []
    for j in range(CMP_STRIDE):
        cols.append(jnp.concatenate(
            [r[pl.ds(j, n_hb, stride=CMP_STRIDE), kind].reshape(n_hb * r.shape[2], r.shape[3]) for r in page_refs],
            axis=0))
    return jnp.concatenate(cols, axis=1)


def _sample_cmp_kernel(pt_ref, *refs, n_pages, G, hg, n_tok, n_top, past_len):
    D = LANES
    page_refs = refs[:n_pages]
    (q_ref, w1tk, w1bk, pek, b1k, w2k, gk, w1tv, w1bv, pev, b1v, w2v, c2s_ref, rep_ref, vis_ref,
     oc_ref, sel_ref) = refs[n_pages:]
    Hk = _page_half_blocks(page_refs, 0).astype(bf16)
    kc = _head_norm(_compress_rows(Hk, w1tk[...], w1bk[...], pek[...], b1k[...], w2k[...], G), gk[...]).astype(bf16)
    Hv = _page_half_blocks(page_refs, 1).astype(bf16)
    vc = _compress_rows(Hv, w1tv[...], w1bv[...], pev[...], b1v[...], w2v[...], G).astype(bf16)
    R = hg * n_tok
    RG = G * R
    nb = c2s_ref.shape[1]
    scale = D ** -0.5
    p = _softmax_masked(_dot_nt(q_ref[0].reshape(RG, D), kc) * scale, vis_ref[...])
    oc_ref[0] = jnp.dot(p.astype(bf16), vc, preferred_element_type=f32).reshape(G, R, D)
    imp = _dot_split(_dot_split_left(rep_ref[...], p), c2s_ref[...])
    imp_t = jnp.concatenate([imp, jnp.zeros((LANES - RG, nb), f32)], axis=0).T
    cur_t = ((past_len + lax.broadcasted_iota(jnp.int32, (1, LANES), 1) % n_tok) // SEL_BLOCK).astype(f32)
    sel = _select_blocks(imp_t, cur_t, n_top, 0).T
    for g in range(G):
        sel_ref[0, g] = sel[g * R:(g + 1) * R]


def _dot_split_left(a, b):
    hi = b.astype(bf16)
    r1 = b - hi.astype(f32)
    mid = r1.astype(bf16)
    lo = (r1 - mid.astype(f32)).astype(bf16)
    return (jnp.dot(a, hi, preferred_element_type=f32) + jnp.dot(a, mid, preferred_element_type=f32)
            + jnp.dot(a, lo, preferred_element_type=f32))


def _page_specs(n_pages, page_rows, kind_pair, layer):
    return [pl.BlockSpec((None, None, page_rows, 2, N_KV_HEADS, LANES),
                         functools.partial(lambda b, pt, p: (layer, pt[b, p], 0, kind_pair, 0, 0), p=p))
            for p in range(n_pages)]


def _sample_cmp(cache_kv, layer, page_table, q_s, cwk, cwv, kc_norm_g, n_tok):
    B, n_pages = page_table.shape
    page_rows = cache_kv.shape[2]
    G, D = N_KV_HEADS, LANES
    hg = N_Q_HEADS // G
    R = hg * n_tok
    past_len = n_pages * page_rows
    L = past_len + n_tok
    nc = (L - CMP_LEN) // CMP_STRIDE + 1
    npg = past_len // CMP_STRIDE
    assert nc <= npg and npg % LANES == 0
    nsel = -(-L // SEL_BLOCK)
    nb = -(-nsel // LANES) * LANES
    c2s = jnp.repeat(_cmp_to_sel(npg, nb, nc, nsel), G, axis=0)
    assert G * R <= LANES and R % n_tok == 0
    rr = np.arange(G * R)
    rep = jnp.asarray((rr[:, None] // R == rr[None, :] // R) & (rr[:, None] % n_tok == rr[None, :] % n_tok),
                      dtype=bf16)
    cc = np.arange(npg * G)
    vis = jnp.asarray((rr[:, None] // R == cc[None, :] % G) & (cc[None, :] // G < nc)
                      & ((cc[None, :] // G) * CMP_STRIDE + CMP_LEN - 1 <= past_len + rr[:, None] % n_tok), dtype=f32)
    gk = kc_norm_g.reshape(1, D)
    consts = list(cwk) + [gk] + list(cwv) + [c2s, rep, vis]
    full = lambda a: pl.BlockSpec(a.shape, lambda b, pt: (0,) * a.ndim)
    est = 2 * n_pages * page_rows * 2 * G * D * 4 + 2 * sum(_nbytes(a.shape, a.dtype) for a in consts) \
        + 6 * G * npg * CMP_STRIDE * D * 4
    return pl.pallas_call(
        functools.partial(_sample_cmp_kernel, n_pages=n_pages, G=G, hg=hg, n_tok=n_tok,
                          n_top=min(SEL_TOP, nsel), past_len=past_len),
        grid_spec=pltpu.PrefetchScalarGridSpec(
            num_scalar_prefetch=1,
            grid=(B,),
            in_specs=_page_specs(n_pages, page_rows, 0, layer)
            + [pl.BlockSpec((1, G, R, D), lambda b, pt: (b, 0, 0, 0))] + [full(a) for a in consts],
            out_specs=[pl.BlockSpec((1, G, R, D), lambda b, pt: (b, 0, 0, 0)),
                       pl.BlockSpec((1, G, R, nb), lambda b, pt: (b, 0, 0, 0))]),
        out_shape=[jax.ShapeDtypeStruct((B, G, R, D), f32),
                   jax.ShapeDtypeStruct((B, G, R, nb), f32)],
        compiler_params=_cparams(("parallel",), est),
        name="sample_cmp",
    )(page_table, *([cache_kv] * n_pages), q_s, *consts)


def _sample_attn_kernel(pt_ref, *refs, n_pages, G, hg, n_tok, past_len):
    D = LANES
    PR = refs[0].shape[0]
    pages2 = [_rows2d(r) for r in refs[:n_pages]]
    (q_ref, gate_ref, knew_ref, vnew_ref, kwnew_ref, vwnew_ref, wnew_ref, win_ref, sel_ref, oc_ref, kblk_ref,
     o_ref, nwin_ref) = refs[n_pages:]
    pitch = 2 * G
    R = hg * n_tok
    n_win = win_ref.shape[0]
    win2 = _rows2d(win_ref)
    c_exp = (D ** -0.5) * math.log2(math.e)
    t_row = past_len + lax.broadcasted_iota(jnp.int32, (R, 1), 0) % n_tok
    new_pos = past_len + lax.broadcasted_iota(jnp.int32, (1, PR), 1)
    new_ok = (new_pos <= t_row) & (new_pos < past_len + n_tok)
    wp_old = past_len - n_win + lax.broadcasted_iota(jnp.int32, (1, n_win), 1)
    vis_old = (wp_old <= t_row) & (wp_old > t_row - WINDOW) & (wp_old >= 0)
    vis_wnew = new_ok & (new_pos > t_row - WINDOW)
    zeros_new = jnp.zeros((PR - knew_ref.shape[2], D), bf16)
    gate = jax.nn.sigmoid(gate_ref[0])

    def softmax(s):
        e = jnp.exp2((s - jnp.max(s, axis=-1, keepdims=True)) * c_exp)
        return (e * (1.0 / jnp.maximum(jnp.sum(e, axis=-1, keepdims=True), 1e-30))).astype(bf16)

    for g in range(G):
        q = q_ref[0, g]
        q_aug = jnp.concatenate([q, (1.0 - sel_ref[0, g]).astype(bf16)], axis=1)
        s_parts = []
        for p in range(n_pages):
            kp = _slab(pages2[p], 0, g, PR, pitch).astype(bf16)
            s_parts.append(_dot_nt(q_aug, jnp.concatenate([kp, kblk_ref[p]], axis=1)))
        k_new = jnp.concatenate([knew_ref[0, g], zeros_new], axis=0)
        s_new = _dot_nt(q_aug, jnp.concatenate([k_new, kblk_ref[n_pages]], axis=1))
        s_parts.append(jnp.where(new_ok, s_new, NEG))
        p_s = softmax(jnp.concatenate(s_parts, axis=1))
        v_new = jnp.concatenate([vnew_ref[0, g], zeros_new], axis=0)
        o_s = jnp.dot(p_s[:, n_pages * PR:], v_new, preferred_element_type=f32)
        for p in range(n_pages):
            vp = _slab(pages2[p], 1, g, PR, pitch).astype(bf16)
            o_s = o_s + jnp.dot(p_s[:, p * PR:(p + 1) * PR], vp, preferred_element_type=f32)
        kw_old = _slab(win2, 0, g, n_win, pitch).astype(bf16)
        vw_old = _slab(win2, 1, g, n_win, pitch).astype(bf16)
        kw_new = jnp.concatenate([kwnew_ref[0, g], zeros_new], axis=0)
        vw_new = jnp.concatenate([vwnew_ref[0, g], zeros_new], axis=0)
        p_w = softmax(jnp.concatenate([jnp.where(vis_old, _dot_nt(q, kw_old), NEG),
                                       jnp.where(vis_wnew, _dot_nt(q, kw_new), NEG)], axis=1))
        o_w = (jnp.dot(p_w[:, :n_win], vw_old, preferred_element_type=f32)
               + jnp.dot(p_w[:, n_win:], vw_new, preferred_element_type=f32))
        gg = gate[g]
        o_ref[0, g] = (oc_ref[0, g] * gg[:, 0:1] + o_s * gg[:, 1:2] + o_w * gg[:, 2:3]).astype(o_ref.dtype)
    nwin_ref[0:n_win - n_tok] = win_ref[n_tok:n_win]
    nwin_ref[n_win - n_tok:n_win] = wnew_ref[...]


def _sample_attn(cache_kv, layer, page_table, q_s, gate_s, new_rows, win_new, state_win_kv, sel, o_c, n_tok):
    B, n_pages = page_table.shape
    page_rows = cache_kv.shape[2]
    G, D = N_KV_HEADS, LANES
    hg = N_Q_HEADS // G
    R = hg * n_tok
    past_len = n_pages * page_rows
    n_win = state_win_kv.shape[2]
    nb = sel.shape[-1]
    P8 = new_rows[0].shape[2]
    assert page_rows % SEL_BLOCK == 0 and nb == LANES and n_win >= n_tok
    kblk = jnp.stack([_key_block_onehot(page_rows, nb, (p * page_rows) // SEL_BLOCK) for p in range(n_pages + 1)])
    per_b = lambda shp: pl.BlockSpec((1,) + shp, lambda b, pt: (b,) + (0,) * len(shp))
    win_spec = pl.BlockSpec((None, None, n_win, 2, G, D), lambda b, pt: (layer, b, 0, 0, 0, 0))
    est = 2 * n_pages * page_rows * 2 * G * D * 4 + 4 * n_win * 2 * G * D * 4 + 40 * R * (past_len + page_rows) * 4
    return pl.pallas_call(
        functools.partial(_sample_attn_kernel, n_pages=n_pages, G=G, hg=hg, n_tok=n_tok, past_len=past_len),
        grid_spec=pltpu.PrefetchScalarGridSpec(
            num_scalar_prefetch=1,
            grid=(B,),
            in_specs=_page_specs(n_pages, page_rows, 1, layer)
            + [per_b((G, R, D)), per_b((G, R, LANES))] + [per_b((G, P8, D))] * 4
            + [pl.BlockSpec((None, n_tok, 2, G, D), lambda b, pt: (b, 0, 0, 0, 0)), win_spec,
               per_b((G, R, nb)), per_b((G, R, D)), pl.BlockSpec(kblk.shape, lambda b, pt: (0, 0, 0))],
            out_specs=[per_b((G, R, D)), win_spec]),
        out_shape=[jax.ShapeDtypeStruct((B, G, R, D), bf16),
                   jax.ShapeDtypeStruct(state_win_kv.shape, state_win_kv.dtype)],
        compiler_params=_cparams(("parallel",), est),
        name="sample_attn",
    )(page_table, *([cache_kv] * n_pages), q_s, gate_s, *new_rows, win_new, state_win_kv, sel, o_c, kblk)


def _pick_tile(n, pref):
    t = min(n, pref)
    while n % t:
        t //= 2
    return t


def _proj_layout(ret_w, qw, kvw):
    lay, off = {}, 0
    for name, w in (("rq", ret_w), ("rk", ret_w), ("rv", ret_w), ("rg", ret_w), ("nq", qw), ("nkv", 6 * kvw)):
        lay[name] = off
        off += w
    lay["width"] = off
    lay["ret_dk"] = ret_w // N_RET_HEADS
    return lay


def _prep_w_in(w_in, d_model, n_main):
    G = N_KV_HEADS
    hg = N_Q_HEADS // G
    w_t = jnp.swapaxes(w_in, 0, 1)
    n_gate = 3 * N_Q_HEADS
    ng = jnp.pad(w_t[n_main:n_main + n_gate].reshape(G, 3 * hg, d_model), ((0, 0), (0, LANES - 3 * hg), (0, 0)))
    return ((w_t, 0, n_main), (w_t, n_main + n_gate, 2 * d_model), (ng.reshape(G * LANES, d_model), 0, G * LANES))


def _in_proj(xn, w_in_parts, tm, tag):
    return [_mm([(xn, 0, xn.shape[1], w)], [], _ep_identity, f32, tm, _pick_tile(n, 512), n_out=n, b_row0=r0,
                name="in_proj_%s%d" % (tag, i)) for i, (w, r0, n) in enumerate(w_in_parts)]


def _dense_tail(x, y_ret, y_nsa, proj_g, wts, tm):
    w_ret_out, w_nsa_out, w_out, norm2_g, w_up, w_down = wts
    M, d_model = x.shape
    d_ff = w_up.shape[1]
    tn = _pick_tile(d_model, 512)
    m = _mm([(y_ret, 0, y_ret.shape[1], w_ret_out), (y_nsa, 0, y_nsa.shape[1], w_nsa_out)],
            [(proj_g, 0), (proj_g, d_model)], _ep_merge, bf16, tm, tn, name="merge")
    h = _mm([(m, 0, d_model, w_out)], [(x, 0)], _ep_residual, f32, tm, tn, name="out_proj")
    hn = _rmsnorm(h, norm2_g, _pick_tile(M, 256))
    u = _mm([(hn, 0, d_model, w_up)], [], _ep_relu2, bf16, tm, _pick_tile(d_ff, 512), name="mlp_up")
    return _mm([(u, 0, d_ff, w_down)], [(h, 0)], _ep_residual, f32, tm, _pick_tile(d_model, 1024),
               tk=_pick_tile(d_ff, 2048), name="mlp_down")


def kernel(x_prompt, x_sample, cache_kv, page_table, state_win_kv, state_ret, norm1_g, w_in, w_ret_out,
           q_norm_g, kc_norm_g, ks_norm_g, kw_norm_g, cmp_pe_k, cmp_w1_k, cmp_b1_k, cmp_w2_k,
           cmp_pe_v, cmp_w1_v, cmp_b1_v, cmp_w2_v, w_nsa_out, w_out, norm2_g, w_up, w_down):
    depth = w_in.shape[0]
    assert depth == 1 and x_prompt.shape[0] == 1
    B, T, d_model = x_prompt.shape
    Bs, Ts, _ = x_sample.shape
    G, D = N_KV_HEADS, LANES
    hg = N_Q_HEADS // G
    qw, kvw = N_Q_HEADS * D, G * D
    ret_w = w_ret_out.shape[1]
    dk = ret_w // N_RET_HEADS
    page_rows = cache_kv.shape[2]
    n_pages = page_table.shape[1]
    past_len = n_pages * page_rows
    lay = _proj_layout(ret_w, qw, kvw)

    l = 0
    w_in_parts = _prep_w_in(w_in[l], d_model, lay["width"])
    wts = (w_ret_out[l], w_nsa_out[l], w_out[l], norm2_g[l], w_up[l], w_down[l])
    cwk = _cmp_weights(cmp_pe_k[l], cmp_w1_k[l], cmp_b1_k[l], cmp_w2_k[l])
    cwv = _cmp_weights(cmp_pe_v[l], cmp_w1_v[l], cmp_b1_v[l], cmp_w2_v[l])

    xp = x_prompt.reshape(T, d_model)
    tm = _pick_tile(T, 1024)
    xn = _rmsnorm(xp, norm1_g[l], _pick_tile(T, 256))
    proj, proj_g, gates = _in_proj(xn, w_in_parts, tm, "p")
    qn, paged, win, kvb = _nsa_prep(proj, lay, q_norm_g[l], ks_norm_g[l], kw_norm_g[l], _pick_tile(T, 256))
    nhb = T // CMP_STRIDE
    half_blocks = (paged[:, :2 * kvw].astype(bf16).reshape(nhb, CMP_STRIDE, 2, G, D)
                   .transpose(2, 3, 0, 1, 4).reshape(2, G, nhb, CMP_STRIDE * D))
    kc = _compress_prompt(half_blocks[0], cwk, kc_norm_g[l], True)
    vc = _compress_prompt(half_blocks[1], cwv, kc_norm_g[l], False)
    y_nsa = _nsa_prompt(qn, gates, kc, vc, kvb)
    y_ret, s_prompt = _ret_prompt(proj, lay, T)
    y_prompt = _dense_tail(xp, y_ret, y_nsa, proj_g, wts, tm).reshape(B, T, d_model)
    new_kv_prompt = paged.reshape(1, B, T, 4, G, D)
    n_keep = min(WINDOW, T)
    new_win_prompt = win[T - n_keep:].reshape(1, B, n_keep, 2, G, D)
    new_ret_prompt = s_prompt.reshape(1, B, N_RET_HEADS, dk, dk)

    Ms = Bs * Ts
    xs = x_sample.reshape(Ms, d_model)
    tms = _pick_tile(Ms, 512)
    xn = _rmsnorm(xs, norm1_g[l], _pick_tile(Ms, 256))
    proj_s, proj_gs, gates = _in_proj(xn, w_in_parts, tms, "s")
    qn, paged_s, win_s, kvb = _nsa_prep(proj_s, lay, q_norm_g[l], ks_norm_g[l], kw_norm_g[l], _pick_tile(Ms, 256))
    P8 = 8
    assert Ts <= P8
    proj8 = jnp.pad(proj_s.reshape(Bs, Ts, lay["width"]), ((0, 0), (0, P8 - Ts), (0, 0)))
    y_ret8, s_sample = _ret_sample(proj8, lay, state_ret[l], past_len, Ts)
    y_ret = y_ret8[:, :Ts].reshape(Ms, ret_w)
    q_s = qn.reshape(Bs, Ts, G, hg, D).transpose(0, 2, 3, 1, 4).reshape(Bs, G, hg * Ts, D)
    gate_s = (gates.reshape(Bs, Ts, G, LANES)[..., :3 * hg]
              .reshape(Bs, Ts, G, hg, 3).transpose(0, 2, 3, 1, 4).reshape(Bs, G, hg * Ts, 3))
    gate_s = jnp.pad(gate_s, ((0, 0), (0, 0), (0, 0), (0, LANES - 3)))
    kvb5 = jnp.pad(kvb.reshape(Bs, Ts, 4, G, D), ((0, 0), (0, P8 - Ts), (0, 0), (0, 0), (0, 0)))
    new_rows = [kvb5[:, :, c].transpose(0, 2, 1, 3) for c in range(4)]
    win_new = win_s.reshape(Bs, Ts, 2, G, D)
    o_c, sel = _sample_cmp(cache_kv, l, page_table, q_s, cwk, cwv, kc_norm_g[l], Ts)
    o_s, new_win_sample = _sample_attn(cache_kv, l, page_table, q_s, gate_s, new_rows, win_new, state_win_kv,
                                       sel, o_c, Ts)
    y_nsa = o_s.reshape(Bs, G, hg, Ts, D).transpose(0, 3, 1, 2, 4).reshape(Ms, qw)
    y_sample = _dense_tail(xs, y_ret, y_nsa, proj_gs, wts, tms).reshape(Bs, Ts, d_model)
    new_kv_sample = paged_s.reshape(1, Bs, Ts, 4, G, D)
    new_ret_sample = s_sample.reshape(1, Bs, N_RET_HEADS, dk, dk)
    return (y_prompt, y_sample, new_kv_prompt, new_win_prompt, new_ret_prompt,
            new_kv_sample, new_win_sample, new_ret_sample)
```

```python
import functools
import math

import numpy as np
import jax
import jax.numpy as jnp
from jax import lax
from jax.experimental import pallas as pl
from jax.experimental.pallas import tpu as pltpu

N_RET_HEADS = 8
RET_CHUNK = 128
ROPE_BASE = 10000.0
N_Q_HEADS = 16
N_KV_HEADS = 4
CMP_LEN = 32
CMP_STRIDE = 16
SEL_BLOCK = 64
SEL_TOP = 16
WINDOW = 512
Q_BLOCK = 256
EPS = 1e-6

LANES = 128
VMEM_BYTES = 64 * 1024 * 1024
NEG = -1e30
SEL_TILE = 1024

f32 = jnp.float32
bf16 = jnp.bfloat16


def _cparams(sem, est_bytes):
    limit = int(min(VMEM_BYTES - (4 << 20), max(est_bytes + (8 << 20), 32 << 20)))
    return pltpu.CompilerParams(dimension_semantics=sem, vmem_limit_bytes=limit)


def _nbytes(shape, dtype):
    return int(np.prod(shape)) * jnp.dtype(dtype).itemsize


def _rmsnorm_kernel(x_ref, g_ref, o_ref):
    x = x_ref[...]
    y = x * lax.rsqrt(jnp.mean(x * x, axis=-1, keepdims=True) + EPS)
    o_ref[...] = (y * g_ref[...]).astype(o_ref.dtype)


def _rmsnorm(x, g, tm):
    M, D = x.shape
    return pl.pallas_call(
        _rmsnorm_kernel,
        grid=(M // tm,),
        in_specs=[pl.BlockSpec((tm, D), lambda i: (i, 0)),
                  pl.BlockSpec((1, D), lambda i: (0, 0))],
        out_specs=pl.BlockSpec((tm, D), lambda i: (i, 0)),
        out_shape=jax.ShapeDtypeStruct((M, D), bf16),
        compiler_params=_cparams(("parallel",), 2 * tm * D * 6),
        name="rmsnorm",
    )(x, g.reshape(1, D))


def _mm_kernel(*refs, n_pairs, n_extra, nk, epilogue, b_rows):
    a = refs[:n_pairs]
    b = refs[n_pairs:2 * n_pairs]
    ex = refs[2 * n_pairs:2 * n_pairs + n_extra]
    o_ref = refs[2 * n_pairs + n_extra]
    if nk == 1:
        dots = [(_dot_nt if b_rows[i] else functools.partial(jnp.dot, preferred_element_type=f32))(
            a[i][...], b[i][...].astype(bf16)) for i in range(n_pairs)]
        o_ref[...] = epilogue(dots, [e[...] for e in ex]).astype(o_ref.dtype)
    else:
        acc_ref = refs[-1]
        k = pl.program_id(2)

        @pl.when(k == 0)
        def _():
            acc_ref[...] = jnp.zeros_like(acc_ref)

        acc_ref[...] += jnp.dot(a[0][...], b[0][...].astype(bf16), preferred_element_type=f32)

        @pl.when(k == nk - 1)
        def _():
            o_ref[...] = epilogue([acc_ref[...]], [e[...] for e in ex]).astype(o_ref.dtype)


def _mm(pairs, extras, epilogue, out_dtype, tm, tn, tk=None, n_out=None, b_row0=None, name="mm"):
    M = pairs[0][0].shape[0]
    N = pairs[0][3].shape[1] if n_out is None else n_out
    K0 = pairs[0][2]
    tk = K0 if tk is None else tk
    nk = K0 // tk
    assert M % tm == 0 and N % tn == 0 and K0 % tk == 0
    assert nk == 1 or (len(pairs) == 1 and b_row0 is None)
    in_specs, args, est = [], [], 0
    for (a, off, K, b) in pairs:
        kb = K if nk == 1 else tk
        assert off % kb == 0
        in_specs.append(pl.BlockSpec((tm, kb), functools.partial(lambda i, j, k, o: (i, o + k), o=off // kb)))
        args.append(a)
        est += 2 * _nbytes((tm, kb), a.dtype)
    for (a, off, K, b) in pairs:
        kb = K if nk == 1 else tk
        if b_row0 is None:
            in_specs.append(pl.BlockSpec((kb, tn), lambda i, j, k: (k, j)))
        else:
            assert b_row0 % 8 == 0 and tn % 8 == 0
            in_specs.append(pl.BlockSpec((pl.Element(tn), pl.Element(kb)),
                                         lambda i, j, k: (pl.multiple_of(b_row0 + j * tn, 8), 0)))
        args.append(b)
        est += 2 * _nbytes((kb, tn), b.dtype) + (0 if b.dtype == bf16 else _nbytes((kb, tn), bf16))
    for (e, off) in extras:
        assert off % tn == 0
        in_specs.append(pl.BlockSpec((tm, tn), functools.partial(lambda i, j, k, o: (i, o + j), o=off // tn)))
        args.append(e)
        est += 2 * _nbytes((tm, tn), e.dtype)
    est += 2 * _nbytes((tm, tn), out_dtype) + (len(pairs) + 2) * tm * tn * 4
    scratch = [pltpu.VMEM((tm, tn), f32)] if nk > 1 else []
    return pl.pallas_call(
        functools.partial(_mm_kernel, n_pairs=len(pairs), n_extra=len(extras), nk=nk, epilogue=epilogue,
                          b_rows=[b_row0 is not None] * len(pairs)),
        grid=(M // tm, N // tn, nk),
        in_specs=in_specs,
        out_specs=pl.BlockSpec((tm, tn), lambda i, j, k: (i, j)),
        out_shape=jax.ShapeDtypeStruct((M, N), out_dtype),
        scratch_shapes=scratch,
        compiler_params=_cparams(("parallel", "parallel", "arbitrary"), est),
        name=name,
    )(*args)


def _ep_identity(dots, ex):
    return dots[0]


def _ep_merge(dots, ex):
    return jax.nn.sigmoid(ex[0]) * dots[0] + jax.nn.sigmoid(ex[1]) * dots[1]


def _ep_residual(dots, ex):
    return ex[0] + dots[0]


def _ep_relu2(dots, ex):
    return jnp.square(jnp.maximum(dots[0], 0.0))


def _head_norm(x, g):
    return x * lax.rsqrt(jnp.mean(x * x, axis=-1, keepdims=True) + EPS) * g


def _nsa_prep_kernel(q_ref, kv01_ref, kv23_ref, kv45_ref, qg_ref, ksg_ref, kwg_ref,
                     qn_ref, paged_ref, win_ref, kvb_ref, *, n_q, n_kv):
    D = LANES
    kvw = n_kv * D
    for h in range(n_q):
        sl = slice(h * D, (h + 1) * D)
        qn_ref[:, sl] = _head_norm(q_ref[:, sl], qg_ref[...]).astype(qn_ref.dtype)
    paged_ref[:, 0:2 * kvw] = kv01_ref[...]
    for g in range(n_kv):
        sl = slice(g * D, (g + 1) * D)
        ks = _head_norm(kv23_ref[:, sl], ksg_ref[...])
        paged_ref[:, 2 * kvw + g * D:2 * kvw + (g + 1) * D] = ks
        kvb_ref[:, sl] = ks.astype(kvb_ref.dtype)
        kw = _head_norm(kv45_ref[:, sl], kwg_ref[...])
        win_ref[:, sl] = kw
        kvb_ref[:, 2 * kvw + g * D:2 * kvw + (g + 1) * D] = kw.astype(kvb_ref.dtype)
    vs = kv23_ref[:, kvw:2 * kvw]
    paged_ref[:, 3 * kvw:4 * kvw] = vs
    kvb_ref[:, kvw:2 * kvw] = vs.astype(kvb_ref.dtype)
    vw = kv45_ref[:, kvw:2 * kvw]
    win_ref[:, kvw:2 * kvw] = vw
    kvb_ref[:, 3 * kvw:4 * kvw] = vw.astype(kvb_ref.dtype)


def _nsa_prep(proj, lay, q_norm_g, ks_norm_g, kw_norm_g, tm):
    M = proj.shape[0]
    n_q, n_kv, D = N_Q_HEADS, N_KV_HEADS, LANES
    qw, kvw = n_q * D, n_kv * D
    assert lay["nq"] % qw == 0 and lay["nkv"] % (2 * kvw) == 0
    kvb0 = lay["nkv"] // (2 * kvw)
    gspec = pl.BlockSpec((1, D), lambda i: (0, 0))
    return pl.pallas_call(
        functools.partial(_nsa_prep_kernel, n_q=n_q, n_kv=n_kv),
        grid=(M // tm,),
        in_specs=[pl.BlockSpec((tm, qw), functools.partial(lambda i, o: (i, o), o=lay["nq"] // qw)),
                  pl.BlockSpec((tm, 2 * kvw), functools.partial(lambda i, o: (i, o), o=kvb0)),
                  pl.BlockSpec((tm, 2 * kvw), functools.partial(lambda i, o: (i, o), o=kvb0 + 1)),
                  pl.BlockSpec((tm, 2 * kvw), functools.partial(lambda i, o: (i, o), o=kvb0 + 2)),
                  gspec, gspec, gspec],
        out_specs=[pl.BlockSpec((tm, qw), lambda i: (i, 0)),
                   pl.BlockSpec((tm, 4 * kvw), lambda i: (i, 0)),
                   pl.BlockSpec((tm, 2 * kvw), lambda i: (i, 0)),
                   pl.BlockSpec((tm, 4 * kvw), lambda i: (i, 0))],
        out_shape=[jax.ShapeDtypeStruct((M, qw), bf16),
                   jax.ShapeDtypeStruct((M, 4 * kvw), f32),
                   jax.ShapeDtypeStruct((M, 2 * kvw), f32),
                   jax.ShapeDtypeStruct((M, 4 * kvw), bf16)],
        compiler_params=_cparams(("parallel",), 2 * tm * (qw * 6 + kvw * 6 * 4 + kvw * 6 * 4 + kvw * 8)),
        name="nsa_prep",
    )(proj, proj, proj, proj, q_norm_g.reshape(1, D), ks_norm_g.reshape(1, D), kw_norm_g.reshape(1, D))


def _gelu_tanh(x):
    return 0.5 * x * (1.0 + jnp.tanh(math.sqrt(2.0 / math.pi) * (x + 0.044715 * (x * x * x))))


def _compress_rows(H, w1t, w1b, pe, b1, w2, step=1):
    half = w1t.shape[0]
    P = jnp.dot(H, w1t, preferred_element_type=f32)
    Q = jnp.dot(H, w1b, preferred_element_type=f32)
    bias = (jnp.dot(pe[:, :half], w1t, preferred_element_type=f32)
            + jnp.dot(pe[:, half:], w1b, preferred_element_type=f32))[0:1] + b1
    hidden = P + pltpu.roll(Q, Q.shape[0] - step, 0) + bias
    return jnp.dot(_gelu_tanh(hidden).astype(bf16), w2, preferred_element_type=f32)


def _compress_kernel(h_ref, w1t_ref, w1b_ref, pe_ref, b1_ref, w2_ref, g_ref, o_ref, *, normalize):
    out = _compress_rows(h_ref[0], w1t_ref[...], w1b_ref[...], pe_ref[...], b1_ref[...], w2_ref[...])
    if normalize:
        out = _head_norm(out, g_ref[...])
    o_ref[0] = out.astype(o_ref.dtype)


def _cmp_weights(pe, w1, b1, w2):
    half = w1.shape[0] // 2
    pe8 = jnp.broadcast_to(pe.reshape(1, -1), (8, pe.size)).astype(bf16)
    return (w1[:half].astype(bf16), w1[half:].astype(bf16), pe8, b1.reshape(1, -1).astype(f32), w2.astype(bf16))


def _compress_prompt(H, cwk, g, normalize):
    G, n, W = H.shape
    w1t, w1b, pe8, b1, w2 = cwk
    hid, D = w2.shape
    full = lambda a: pl.BlockSpec(a.shape, lambda i: (0,) * a.ndim)
    g2 = g.reshape(1, D)
    return pl.pallas_call(
        functools.partial(_compress_kernel, normalize=normalize),
        grid=(G,),
        in_specs=[pl.BlockSpec((1, n, W), lambda i: (i, 0, 0)), full(w1t), full(w1b), full(pe8), full(b1), full(w2),
                  full(g2)],
        out_specs=pl.BlockSpec((1, n, D), lambda i: (i, 0, 0)),
        out_shape=jax.ShapeDtypeStruct((G, n, D), bf16),
        compiler_params=_cparams(("parallel",), 2 * (n * W * 2 + 2 * W * hid * 2) + 6 * n * hid * 4),
        name="compress_prompt",
    )(H, w1t, w1b, pe8, b1, w2, g2)


def _dot_nt(a, b):
    return lax.dot_general(a, b, (((1,), (1,)), ((), ())), preferred_element_type=f32)


def _dot_split(a, b):
    hi = a.astype(bf16)
    r1 = a - hi.astype(f32)
    mid = r1.astype(bf16)
    lo = (r1 - mid.astype(f32)).astype(bf16)
    return (jnp.dot(hi, b, preferred_element_type=f32) + jnp.dot(mid, b, preferred_element_type=f32)
            + jnp.dot(lo, b, preferred_element_type=f32))


def _softmax_masked(s, maskf):
    vis = maskf > 0.0
    sm = jnp.where(vis, s, NEG)
    m = jnp.max(sm, axis=-1, keepdims=True)
    e = jnp.where(vis, jnp.exp(sm - m), 0.0)
    return e / jnp.maximum(jnp.sum(e, axis=-1, keepdims=True), 1e-30)


def _select_blocks(imp, cur, n_top, axis):
    nb = imp.shape[axis]
    blk = lax.broadcasted_iota(jnp.int32, imp.shape, axis).astype(f32)
    forced = jnp.where(blk == 0.0, 1.0, 0.0) + jnp.where(blk == cur, 1.0, 0.0) + jnp.where(blk == cur - 1.0, 1.0, 0.0)
    score = jnp.where(forced > 0.0, jnp.inf, jnp.where(blk <= cur, imp, -jnp.inf))
    sel = jnp.zeros(imp.shape, f32)
    for _ in range(n_top):
        mx = jnp.max(score, axis=axis, keepdims=True)
        first = jnp.min(jnp.where(score == mx, blk, float(nb)), axis=axis, keepdims=True)
        hit = blk == first
        sel = jnp.where(hit, 1.0, sel)
        score = jnp.where(hit, -jnp.inf, score)
    return sel


def _nsa_prompt_kernel(q_ref, gate_ref, kc_ref, vc_ref, ks_ref, vs_ref, kw_ref, vw_ref, c2s_ref, kblk_ref, wbias_ref,
                       o_ref, *, hg, n_top, tk):
    D = LANES
    QB = q_ref.shape[0]
    T = ks_ref.shape[0]
    nc = kc_ref.shape[1]
    R = hg * QB
    c_exp = (D ** -0.5) * math.log2(math.e)
    qb = pl.program_id(1)
    s0 = qb * QB
    q = jnp.concatenate([q_ref[:, h * D:(h + 1) * D] for h in range(hg)], axis=0)
    t_row = s0 + (lax.broadcasted_iota(jnp.int32, (R, 1), 0) & (QB - 1))

    kc_end = lax.broadcasted_iota(jnp.int32, (1, nc), 1) * CMP_STRIDE + (CMP_LEN - 1)
    s = jnp.where(kc_end <= t_row, _dot_nt(q, kc_ref[0]), NEG)
    e = jnp.exp2((s - jnp.max(s, axis=-1, keepdims=True)) * c_exp)
    row_ok = jnp.where(t_row >= CMP_LEN - 1, 1.0, 0.0)
    inv = row_ok / jnp.maximum(jnp.sum(e, axis=-1, keepdims=True), 1e-30)
    o_c = jnp.dot(e.astype(bf16), vc_ref[0], preferred_element_type=f32) * inv

    wl = wbias_ref.shape[2]
    w0 = pl.multiple_of(jnp.clip(s0 - WINDOW, 0, T - wl), QB)
    bias = wbias_ref[jnp.minimum(qb, wbias_ref.shape[0] - 1)]
    sw = _dot_nt(q, kw_ref[pl.ds(w0, wl), :])
    sw = jnp.concatenate([sw[h * QB:(h + 1) * QB] + bias for h in range(hg)], axis=0)
    ew = jnp.exp2((sw - jnp.max(sw, axis=-1, keepdims=True)) * c_exp)
    o_w = (jnp.dot(ew.astype(bf16), vw_ref[pl.ds(w0, wl), :], preferred_element_type=f32)
           * (1.0 / jnp.maximum(jnp.sum(ew, axis=-1, keepdims=True), 1e-30)))

    p_sum = e[0:QB] * inv[0:QB]
    for h in range(1, hg):
        p_sum = p_sum + e[h * QB:(h + 1) * QB] * inv[h * QB:(h + 1) * QB]
    imp_t = _dot_split(p_sum, c2s_ref[...]).T
    cur_t = ((s0 + lax.broadcasted_iota(jnp.int32, (1, QB), 1)) // SEL_BLOCK).astype(f32)
    not_sel = (1.0 - _select_blocks(imp_t, cur_t, n_top, 0)).T.astype(bf16)

    q_aug = jnp.concatenate([q, jnp.concatenate([not_sel] * hg, axis=0)], axis=1)
    col_pos = lax.broadcasted_iota(jnp.int32, (1, tk), 1)

    def sel_tile(kt, carry, causal):
        m, l, acc = carry
        k0 = pl.multiple_of(kt * tk, tk)
        s = _dot_nt(q_aug, jnp.concatenate([ks_ref[pl.ds(k0, tk), :], kblk_ref[pl.ds(k0, tk), :]], axis=1))
        if causal:
            s = jnp.where(col_pos + k0 <= t_row, s, NEG)
        m_new = jnp.maximum(m, jnp.max(s, axis=-1, keepdims=True))
        alpha = jnp.exp2((m - m_new) * c_exp)
        e = jnp.exp2((s - m_new) * c_exp)
        l = alpha * l + jnp.sum(e, axis=-1, keepdims=True)
        acc = alpha * acc + jnp.dot(e.astype(bf16), vs_ref[pl.ds(k0, tk), :], preferred_element_type=f32)
        return m_new, l, acc

    last = (s0 + QB - 1) // tk
    init = (jnp.full((R, 1), NEG, f32), jnp.zeros((R, 1), f32), jnp.zeros((R, D), f32))
    carry = lax.fori_loop(0, last, functools.partial(sel_tile, causal=False), init)
    _, l, acc = sel_tile(last, carry, True)
    o_s = acc * (1.0 / jnp.maximum(l, 1e-30))

    gate = jax.nn.sigmoid(gate_ref[...])
    for h in range(hg):
        rows = slice(h * QB, (h + 1) * QB)
        o = (o_c[rows] * gate[:, 3 * h:3 * h + 1] + o_s[rows] * gate[:, 3 * h + 1:3 * h + 2]
             + o_w[rows] * gate[:, 3 * h + 2:3 * h + 3])
        o_ref[:, h * D:(h + 1) * D] = o.astype(o_ref.dtype)


def _cmp_to_sel(nc_pad, nsel_pad, nc, nsel):
    cs = np.arange(nc_pad)[:, None] * CMP_STRIDE
    bs = np.arange(nsel_pad)[None, :] * SEL_BLOCK
    ov = np.clip(np.minimum(cs + CMP_LEN, bs + SEL_BLOCK) - np.maximum(cs, bs), 0, None).astype(np.float32) / CMP_LEN
    ov[nc:, :] = 0.0
    ov[:, nsel:] = 0.0
    return jnp.asarray(ov, dtype=bf16)


def _key_block_onehot(n_keys, nb, first_block=0):
    blk = first_block + np.arange(n_keys)[:, None] // SEL_BLOCK
    return jnp.asarray(np.where(blk == np.arange(nb)[None, :], NEG, 0.0), dtype=bf16)


def _window_bias(QB, wl, n_off):
    d = (np.arange(n_off)[:, None, None] * QB + np.arange(QB)[None, :, None] - np.arange(wl)[None, None, :])
    return jnp.asarray(np.where((d >= 0) & (d < WINDOW), 0.0, NEG), dtype=f32)


def _nsa_prompt(qn, gates, kc, vc, kvb):
    T = qn.shape[0]
    G, D = N_KV_HEADS, LANES
    hg = N_Q_HEADS // G
    QB = Q_BLOCK
    nc_pad = kc.shape[1]
    nc = (T - CMP_LEN) // CMP_STRIDE + 1
    nsel = -(-T // SEL_BLOCK)
    nb = -(-nsel // LANES) * LANES
    c2s = _cmp_to_sel(nc_pad, nb, nc, nsel)
    kblk = _key_block_onehot(T, nb)
    tk = min(SEL_TILE, T)
    wl = WINDOW + QB
    assert T % tk == 0 and T % QB == 0 and WINDOW % QB == 0 and T >= wl and tk % QB == 0
    wbias = _window_bias(QB, wl, WINDOW // QB + 1)
    est = 2 * (4 * T * D * 2 + T * nb * 2 + wbias.size * 4) + 12 * hg * QB * max(tk, nc_pad, wl) * 4
    return pl.pallas_call(
        functools.partial(_nsa_prompt_kernel, hg=hg, n_top=min(SEL_TOP, nsel), tk=tk),
        grid=(G, T // QB),
        in_specs=[pl.BlockSpec((QB, hg * D), lambda g, i: (i, g)),
                  pl.BlockSpec((QB, LANES), lambda g, i: (i, g)),
                  pl.BlockSpec((1, nc_pad, D), lambda g, i: (g, 0, 0)),
                  pl.BlockSpec((1, nc_pad, D), lambda g, i: (g, 0, 0)),
                  pl.BlockSpec((T, D), lambda g, i: (0, g)),
                  pl.BlockSpec((T, D), functools.partial(lambda g, i, o: (0, o + g), o=G)),
                  pl.BlockSpec((T, D), functools.partial(lambda g, i, o: (0, o + g), o=2 * G)),
                  pl.BlockSpec((T, D), functools.partial(lambda g, i, o: (0, o + g), o=3 * G)),
                  pl.BlockSpec((nc_pad, nb), lambda g, i: (0, 0)),
                  pl.BlockSpec((T, nb), lambda g, i: (0, 0)),
                  pl.BlockSpec(wbias.shape, lambda g, i: (0, 0, 0))],
        out_specs=pl.BlockSpec((QB, hg * D), lambda g, i: (i, g)),
        out_shape=jax.ShapeDtypeStruct((T, N_Q_HEADS * D), bf16),
        compiler_params=_cparams(("parallel", "parallel"), est),
        name="nsa_prompt",
    )(qn, gates, kc, vc, kvb, kvb, kvb, kvb, c2s, kblk, wbias)


def _ret_tables(C, n_valid):
    lg = np.log1p(-np.exp2(-5.0 - np.arange(N_RET_HEADS, dtype=np.float64)))
    i = np.arange(C, dtype=np.float64)
    diff = i[:, None] - i[None, :]
    valid = (i < n_valid)
    dmask = np.where(diff >= 0, np.exp(lg[:, None, None] * np.maximum(diff, 0.0)), 0.0)
    dmask = dmask * (valid[:, None] & valid[None, :])
    q_dec = np.exp(lg[:, None] * (i[None, :] + 1.0)) * valid[None, :]
    k_dec = np.exp(lg[:, None] * (n_valid - 1.0 - i[None, :])) * valid[None, :]
    c_dec = np.exp(lg * n_valid)
    as32 = lambda a: jnp.asarray(a, dtype=f32)
    return (as32(dmask), as32(q_dec[:, :, None]), as32(k_dec[:, :, None]),
            as32(np.broadcast_to(c_dec[:, None, None], (N_RET_HEADS, 1, LANES))))


def _rope_tables(pos, half):
    inv = jnp.power(ROPE_BASE, -jnp.arange(half, dtype=f32) / half)
    ang = pos.astype(f32)[:, None] * inv[None, :]
    return jnp.cos(ang), jnp.sin(ang)


def _rotary(x, cos, sin):
    half = cos.shape[-1]
    x1, x2 = x[:, :half], x[:, half:]
    return jnp.concatenate([x1 * cos - x2 * sin, x1 * sin + x2 * cos], axis=-1)


def _ret_chunk(q_raw, k_raw, v, rg, cos, sin, dmask, q_dec, k_dec, c_dec, S):
    dk = q_raw.shape[-1]
    q = _rotary(q_raw, cos, sin)
    k = _rotary(k_raw, cos, sin) * (dk ** -0.5)
    vb = v.astype(bf16)
    inner = _dot_nt(q.astype(bf16), k.astype(bf16)) * dmask
    o = (jnp.dot(inner.astype(bf16), vb, preferred_element_type=f32)
         + jnp.dot((q * q_dec).astype(bf16), S.astype(bf16), preferred_element_type=f32))
    S_new = S * c_dec + lax.dot_general((k * k_dec).astype(bf16), vb, (((0,), (0,)), ((), ())),
                                        preferred_element_type=f32)
    y = o * lax.rsqrt(jnp.mean(o * o, axis=-1, keepdims=True) + EPS)
    return y * (rg * jax.nn.sigmoid(rg)), S_new


def _ret_prompt_kernel(q_ref, k_ref, v_ref, g_ref, cos_ref, sin_ref, dm_ref, qd_ref, kd_ref, cd_ref,
                       y_ref, s_out_ref, s_ref, *, hp, dk):
    c = pl.program_id(1)

    @pl.when(c == 0)
    def _():
        s_ref[...] = jnp.zeros_like(s_ref)

    for h in range(hp):
        sl = slice(h * dk, (h + 1) * dk)
        y, S_new = _ret_chunk(q_ref[:, sl], k_ref[:, sl], v_ref[:, sl], g_ref[:, sl], cos_ref[...], sin_ref[...],
                              dm_ref[h], qd_ref[h], kd_ref[h], cd_ref[h][:, 0:1], s_ref[h])
        y_ref[:, sl] = y.astype(y_ref.dtype)
        s_ref[h] = S_new

    @pl.when(c == pl.num_programs(1) - 1)
    def _():
        s_out_ref[...] = s_ref[...]


def _ret_prompt(proj, lay, T):
    H = N_RET_HEADS
    dk = lay["ret_dk"]
    C = min(RET_CHUNK, T)
    hp = next(n for n in (8, 4, 2, 1) if H % n == 0)
    w = hp * dk
    assert T % C == 0 and all(lay[n] % w == 0 for n in ("rq", "rk", "rv", "rg"))
    cos, sin = _rope_tables(jnp.arange(T), dk // 2)
    dmask, q_dec, k_dec, c_dec = _ret_tables(C, C)
    col = lambda name: functools.partial(lambda h, c, o: (c, o + h), o=lay[name] // w)
    tab = lambda shp: pl.BlockSpec((hp,) + shp, lambda h, c: (h, 0, 0))
    return pl.pallas_call(
        functools.partial(_ret_prompt_kernel, hp=hp, dk=dk),
        grid=(H // hp, T // C),
        in_specs=[pl.BlockSpec((C, w), col("rq")), pl.BlockSpec((C, w), col("rk")),
                  pl.BlockSpec((C, w), col("rv")), pl.BlockSpec((C, w), col("rg")),
                  pl.BlockSpec((C, dk // 2), lambda h, c: (c, 0)), pl.BlockSpec((C, dk // 2), lambda h, c: (c, 0)),
                  tab((C, C)), tab((C, 1)), tab((C, 1)), tab((1, LANES))],
        out_specs=[pl.BlockSpec((C, w), lambda h, c: (c, h)),
                   pl.BlockSpec((hp, dk, dk), lambda h, c: (h, 0, 0))],
        out_shape=[jax.ShapeDtypeStruct((T, H * dk), bf16),
                   jax.ShapeDtypeStruct((H, dk, dk), f32)],
        scratch_shapes=[pltpu.VMEM((hp, dk, dk), f32)],
        compiler_params=_cparams(("parallel", "arbitrary"), hp * (16 * C * dk * 4 + 4 * dk * dk * 4)),
        name="ret_prompt",
    )(proj, proj, proj, proj, cos, sin, dmask, q_dec, k_dec, c_dec)


def _ret_sample_kernel(q_ref, k_ref, v_ref, g_ref, cos_ref, sin_ref, dm_ref, qd_ref, kd_ref, cd_ref, s0_ref,
                       y_ref, s_out_ref, *, n_heads, dk):
    for h in range(n_heads):
        sl = slice(h * dk, (h + 1) * dk)
        y, S_new = _ret_chunk(q_ref[0][:, sl], k_ref[0][:, sl], v_ref[0][:, sl], g_ref[0][:, sl],
                              cos_ref[...], sin_ref[...], dm_ref[h], qd_ref[h], kd_ref[h], cd_ref[h][:, 0:1],
                              s0_ref[0, h])
        y_ref[0, :, sl] = y.astype(y_ref.dtype)
        s_out_ref[0, h] = S_new


def _ret_sample(proj8, lay, state, past_len, n_tok):
    B, P, _ = proj8.shape
    H = N_RET_HEADS
    dk = lay["ret_dk"]
    w = H * dk
    assert all(lay[n] % w == 0 for n in ("rq", "rk", "rv", "rg"))
    cos, sin = _rope_tables(past_len + jnp.arange(P), dk // 2)
    dmask, q_dec, k_dec, c_dec = _ret_tables(P, n_tok)
    col = lambda name: functools.partial(lambda b, o: (b, 0, o), o=lay[name] // w)
    full = lambda a: pl.BlockSpec(a.shape, lambda b: (0,) * a.ndim)
    return pl.pallas_call(
        functools.partial(_ret_sample_kernel, n_heads=H, dk=dk),
        grid=(B,),
        in_specs=[pl.BlockSpec((1, P, w), col("rq")), pl.BlockSpec((1, P, w), col("rk")),
                  pl.BlockSpec((1, P, w), col("rv")), pl.BlockSpec((1, P, w), col("rg")),
                  full(cos), full(sin), full(dmask), full(q_dec), full(k_dec), full(c_dec),
                  pl.BlockSpec((1, H, dk, dk), lambda b: (b, 0, 0, 0))],
        out_specs=[pl.BlockSpec((1, P, w), lambda b: (b, 0, 0)),
                   pl.BlockSpec((1, H, dk, dk), lambda b: (b, 0, 0, 0))],
        out_shape=[jax.ShapeDtypeStruct((B, P, w), bf16),
                   jax.ShapeDtypeStruct((B, H, dk, dk), f32)],
        compiler_params=_cparams(("parallel",), 4 * H * dk * dk * 4 + 16 * P * w * 4),
        name="ret_sample",
    )(proj8, proj8, proj8, proj8, cos, sin, dmask, q_dec, k_dec, c_dec, state)


def _rows2d(r):
    return r.reshape(r.shape[0] * r.shape[1] * r.shape[2], r.shape[3])


def _slab(r2, kind, g, n_rows, pitch):
    return r2[pl.ds(kind * (pitch // 2) + g, n_rows, stride=pitch), :]


def _page_half_blocks(page_refs, kind):
    n_hb = page_refs[0].shape[0] // CMP_STRIDE
    cols = []
    for j in range(CMP_STRIDE):
        cols.append(jnp.concatenate(
            [r[pl.ds(j, n_hb, stride=CMP_STRIDE), kind].reshape(n_hb * r.shape[2], r.shape[3]) for r in page_refs],
            axis=0))
    return jnp.concatenate(cols, axis=1)


def _sample_cmp_kernel(pt_ref, *refs, n_pages, G, hg, n_tok, n_top, past_len):
    D = LANES
    page_refs = refs[:n_pages]
    (q_ref, w1tk, w1bk, pek, b1k, w2k, gk, w1tv, w1bv, pev, b1v, w2v, c2s_ref, rep_ref, vis_ref,
     oc_ref, sel_ref) = refs[n_pages:]
    Hk = _page_half_blocks(page_refs, 0).astype(bf16)
    kc = _head_norm(_compress_rows(Hk, w1tk[...], w1bk[...], pek[...], b1k[...], w2k[...], G), gk[...]).astype(bf16)
    Hv = _page_half_blocks(page_refs, 1).astype(bf16)
    vc = _compress_rows(Hv, w1tv[...], w1bv[...], pev[...], b1v[...], w2v[...], G).astype(bf16)
    R = hg * n_tok
    RG = G * R
    nb = c2s_ref.shape[1]
    scale = D ** -0.5
    p = _softmax_masked(_dot_nt(q_ref[0].reshape(RG, D), kc) * scale, vis_ref[...])
    oc_ref[0] = jnp.dot(p.astype(bf16), vc, preferred_element_type=f32).reshape(G, R, D)
    imp = _dot_split(_dot_split_left(rep_ref[...], p), c2s_ref[...])
    imp_t = jnp.concatenate([imp, jnp.zeros((LANES - RG, nb), f32)], axis=0).T
    cur_t = ((past_len + lax.broadcasted_iota(jnp.int32, (1, LANES), 1) % n_tok) // SEL_BLOCK).astype(f32)
    sel = _select_blocks(imp_t, cur_t, n_top, 0).T
    for g in range(G):
        sel_ref[0, g] = sel[g * R:(g + 1) * R]


def _dot_split_left(a, b):
    hi = b.astype(bf16)
    r1 = b - hi.astype(f32)
    mid = r1.astype(bf16)
    lo = (r1 - mid.astype(f32)).astype(bf16)
    return (jnp.dot(a, hi, preferred_element_type=f32) + jnp.dot(a, mid, preferred_element_type=f32)
            + jnp.dot(a, lo, preferred_element_type=f32))


def _page_specs(n_pages, page_rows, kind_pair, layer):
    return [pl.BlockSpec((None, None, page_rows, 2, N_KV_HEADS, LANES),
                         functools.partial(lambda b, pt, p: (layer, pt[b, p], 0, kind_pair, 0, 0), p=p))
            for p in range(n_pages)]


def _sample_cmp(cache_kv, layer, page_table, q_s, cwk, cwv, kc_norm_g, n_tok):
    B, n_pages = page_table.shape
    page_rows = cache_kv.shape[2]
    G, D = N_KV_HEADS, LANES
    hg = N_Q_HEADS // G
    R = hg * n_tok
    past_len = n_pages * page_rows
    L = past_len + n_tok
    nc = (L - CMP_LEN) // CMP_STRIDE + 1
    npg = past_len // CMP_STRIDE
    assert nc <= npg and npg % LANES == 0
    nsel = -(-L // SEL_BLOCK)
    nb = -(-nsel // LANES) * LANES
    c2s = jnp.repeat(_cmp_to_sel(npg, nb, nc, nsel), G, axis=0)
    assert G * R <= LANES and R % n_tok == 0
    rr = np.arange(G * R)
    rep = jnp.asarray((rr[:, None] // R == rr[None, :] // R) & (rr[:, None] % n_tok == rr[None, :] % n_tok),
                      dtype=bf16)
    cc = np.arange(npg * G)
    vis = jnp.asarray((rr[:, None] // R == cc[None, :] % G) & (cc[None, :] // G < nc)
                      & ((cc[None, :] // G) * CMP_STRIDE + CMP_LEN - 1 <= past_len + rr[:, None] % n_tok), dtype=f32)
    gk = kc_norm_g.reshape(1, D)
    consts = list(cwk) + [gk] + list(cwv) + [c2s, rep, vis]
    full = lambda a: pl.BlockSpec(a.shape, lambda b, pt: (0,) * a.ndim)
    est = 2 * n_pages * page_rows * 2 * G * D * 4 + 2 * sum(_nbytes(a.shape, a.dtype) for a in consts) \
        + 6 * G * npg * CMP_STRIDE * D * 4
    return pl.pallas_call(
        functools.partial(_sample_cmp_kernel, n_pages=n_pages, G=G, hg=hg, n_tok=n_tok,
                          n_top=min(SEL_TOP, nsel), past_len=past_len),
        grid_spec=pltpu.PrefetchScalarGridSpec(
            num_scalar_prefetch=1,
            grid=(B,),
            in_specs=_page_specs(n_pages, page_rows, 0, layer)
            + [pl.BlockSpec((1, G, R, D), lambda b, pt: (b, 0, 0, 0))] + [full(a) for a in consts],
            out_specs=[pl.BlockSpec((1, G, R, D), lambda b, pt: (b, 0, 0, 0)),
                       pl.BlockSpec((1, G, R, nb), lambda b, pt: (b, 0, 0, 0))]),
        out_shape=[jax.ShapeDtypeStruct((B, G, R, D), f32),
                   jax.ShapeDtypeStruct((B, G, R, nb), f32)],
        compiler_params=_cparams(("parallel",), est),
        name="sample_cmp",
    )(page_table, *([cache_kv] * n_pages), q_s, *consts)


def _sample_attn_kernel(pt_ref, *refs, n_pages, G, hg, n_tok, past_len):
    D = LANES
    PR = refs[0].shape[0]
    pages2 = [_rows2d(r) for r in refs[:n_pages]]
    (q_ref, gate_ref, knew_ref, vnew_ref, kwnew_ref, vwnew_ref, wnew_ref, win_ref, sel_ref, oc_ref, kblk_ref,
     o_ref, nwin_ref) = refs[n_pages:]
    pitch = 2 * G
    R = hg * n_tok
    n_win = win_ref.shape[0]
    win2 = _rows2d(win_ref)
    c_exp = (D ** -0.5) * math.log2(math.e)
    t_row = past_len + lax.broadcasted_iota(jnp.int32, (R, 1), 0) % n_tok
    new_pos = past_len + lax.broadcasted_iota(jnp.int32, (1, PR), 1)
    new_ok = (new_pos <= t_row) & (new_pos < past_len + n_tok)
    wp_old = past_len - n_win + lax.broadcasted_iota(jnp.int32, (1, n_win), 1)
    vis_old = (wp_old <= t_row) & (wp_old > t_row - WINDOW) & (wp_old >= 0)
    vis_wnew = new_ok & (new_pos > t_row - WINDOW)
    zeros_new = jnp.zeros((PR - knew_ref.shape[2], D), bf16)
    gate = jax.nn.sigmoid(gate_ref[0])

    def softmax(s):
        e = jnp.exp2((s - jnp.max(s, axis=-1, keepdims=True)) * c_exp)
        return (e * (1.0 / jnp.maximum(jnp.sum(e, axis=-1, keepdims=True), 1e-30))).astype(bf16)

    for g in range(G):
        q = q_ref[0, g]
        q_aug = jnp.concatenate([q, (1.0 - sel_ref[0, g]).astype(bf16)], axis=1)
        s_parts = []
        for p in range(n_pages):
            kp = _slab(pages2[p], 0, g, PR, pitch).astype(bf16)
            s_parts.append(_dot_nt(q_aug, jnp.concatenate([kp, kblk_ref[p]], axis=1)))
        k_new = jnp.concatenate([knew_ref[0, g], zeros_new], axis=0)
        s_new = _dot_nt(q_aug, jnp.concatenate([k_new, kblk_ref[n_pages]], axis=1))
        s_parts.append(jnp.where(new_ok, s_new, NEG))
        p_s = softmax(jnp.concatenate(s_parts, axis=1))
        v_new = jnp.concatenate([vnew_ref[0, g], zeros_new], axis=0)
        o_s = jnp.dot(p_s[:, n_pages * PR:], v_new, preferred_element_type=f32)
        for p in range(n_pages):
            vp = _slab(pages2[p], 1, g, PR, pitch).astype(bf16)
            o_s = o_s + jnp.dot(p_s[:, p * PR:(p + 1) * PR], vp, preferred_element_type=f32)
        kw_old = _slab(win2, 0, g, n_win, pitch).astype(bf16)
        vw_old = _slab(win2, 1, g, n_win, pitch).astype(bf16)
        kw_new = jnp.concatenate([kwnew_ref[0, g], zeros_new], axis=0)
        vw_new = jnp.concatenate([vwnew_ref[0, g], zeros_new], axis=0)
        p_w = softmax(jnp.concatenate([jnp.where(vis_old, _dot_nt(q, kw_old), NEG),
                                       jnp.where(vis_wnew, _dot_nt(q, kw_new), NEG)], axis=1))
        o_w = (jnp.dot(p_w[:, :n_win], vw_old, preferred_element_type=f32)
               + jnp.dot(p_w[:, n_win:], vw_new, preferred_element_type=f32))
        gg = gate[g]
        o_ref[0, g] = (oc_ref[0, g] * gg[:, 0:1] + o_s * gg[:, 1:2] + o_w * gg[:, 2:3]).astype(o_ref.dtype)
    nwin_ref[0:n_win - n_tok] = win_ref[n_tok:n_win]
    nwin_ref[n_win - n_tok:n_win] = wnew_ref[...]


def _sample_attn(cache_kv, layer, page_table, q_s, gate_s, new_rows, win_new, state_win_kv, sel, o_c, n_tok):
    B, n_pages = page_table.shape
    page_rows = cache_kv.shape[2]
    G, D = N_KV_HEADS, LANES
    hg = N_Q_HEADS // G
    R = hg * n_tok
    past_len = n_pages * page_rows
    n_win = state_win_kv.shape[2]
    nb = sel.shape[-1]
    P8 = new_rows[0].shape[2]
    assert page_rows % SEL_BLOCK == 0 and nb == LANES and n_win >= n_tok
    kblk = jnp.stack([_key_block_onehot(page_rows, nb, (p * page_rows) // SEL_BLOCK) for p in range(n_pages + 1)])
    per_b = lambda shp: pl.BlockSpec((1,) + shp, lambda b, pt: (b,) + (0,) * len(shp))
    win_spec = pl.BlockSpec((None, None, n_win, 2, G, D), lambda b, pt: (layer, b, 0, 0, 0, 0))
    est = 2 * n_pages * page_rows * 2 * G * D * 4 + 4 * n_win * 2 * G * D * 4 + 40 * R * (past_len + page_rows) * 4
    return pl.pallas_call(
        functools.partial(_sample_attn_kernel, n_pages=n_pages, G=G, hg=hg, n_tok=n_tok, past_len=past_len),
        grid_spec=pltpu.PrefetchScalarGridSpec(
            num_scalar_prefetch=1,
            grid=(B,),
            in_specs=_page_specs(n_pages, page_rows, 1, layer)
            + [per_b((G, R, D)), per_b((G, R, LANES))] + [per_b((G, P8, D))] * 4
            + [pl.BlockSpec((None, n_tok, 2, G, D), lambda b, pt: (b, 0, 0, 0, 0)), win_spec,
               per_b((G, R, nb)), per_b((G, R, D)), pl.BlockSpec(kblk.shape, lambda b, pt: (0, 0, 0))],
            out_specs=[per_b((G, R, D)), win_spec]),
        out_shape=[jax.ShapeDtypeStruct((B, G, R, D), bf16),
                   jax.ShapeDtypeStruct(state_win_kv.shape, state_win_kv.dtype)],
        compiler_params=_cparams(("parallel",), est),
        name="sample_attn",
    )(page_table, *([cache_kv] * n_pages), q_s, gate_s, *new_rows, win_new, state_win_kv, sel, o_c, kblk)


def _pick_tile(n, pref):
    t = min(n, pref)
    while n % t:
        t //= 2
    return t


def _proj_layout(ret_w, qw, kvw):
    lay, off = {}, 0
    for name, w in (("rq", ret_w), ("rk", ret_w), ("rv", ret_w), ("rg", ret_w), ("nq", qw), ("nkv", 6 * kvw)):
        lay[name] = off
        off += w
    lay["width"] = off
    lay["ret_dk"] = ret_w // N_RET_HEADS
    return lay


def _prep_w_in(w_in, d_model, n_main):
    G = N_KV_HEADS
    hg = N_Q_HEADS // G
    w_t = jnp.swapaxes(w_in, 0, 1)
    n_gate = 3 * N_Q_HEADS
    ng = jnp.pad(w_t[n_main:n_main + n_gate].reshape(G, 3 * hg, d_model), ((0, 0), (0, LANES - 3 * hg), (0, 0)))
    return ((w_t, 0, n_main), (w_t, n_main + n_gate, 2 * d_model), (ng.reshape(G * LANES, d_model), 0, G * LANES))


def _in_proj(xn, w_in_parts, tm, tag):
    return [_mm([(xn, 0, xn.shape[1], w)], [], _ep_identity, f32, tm, _pick_tile(n, 512), n_out=n, b_row0=r0,
                name="in_proj_%s%d" % (tag, i)) for i, (w, r0, n) in enumerate(w_in_parts)]


def _dense_tail(x, y_ret, y_nsa, proj_g, wts, tm):
    w_ret_out, w_nsa_out, w_out, norm2_g, w_up, w_down = wts
    M, d_model = x.shape
    d_ff = w_up.shape[1]
    tn = _pick_tile(d_model, 512)
    m = _mm([(y_ret, 0, y_ret.shape[1], w_ret_out), (y_nsa, 0, y_nsa.shape[1], w_nsa_out)],
            [(proj_g, 0), (proj_g, d_model)], _ep_merge, bf16, tm, tn, name="merge")
    h = _mm([(m, 0, d_model, w_out)], [(x, 0)], _ep_residual, f32, tm, tn, name="out_proj")
    hn = _rmsnorm(h, norm2_g, _pick_tile(M, 256))
    u = _mm([(hn, 0, d_model, w_up)], [], _ep_relu2, bf16, tm, _pick_tile(d_ff, 512), name="mlp_up")
    return _mm([(u, 0, d_ff, w_down)], [(h, 0)], _ep_residual, f32, tm, _pick_tile(d_model, 1024),
               tk=_pick_tile(d_ff, 2048), name="mlp_down")


def kernel(x_prompt, x_sample, cache_kv, page_table, state_win_kv, state_ret, norm1_g, w_in, w_ret_out,
           q_norm_g, kc_norm_g, ks_norm_g, kw_norm_g, cmp_pe_k, cmp_w1_k, cmp_b1_k, cmp_w2_k,
           cmp_pe_v, cmp_w1_v, cmp_b1_v, cmp_w2_v, w_nsa_out, w_out, norm2_g, w_up, w_down):
    depth = w_in.shape[0]
    assert depth == 1 and x_prompt.shape[0] == 1
    B, T, d_model = x_prompt.shape
    Bs, Ts, _ = x_sample.shape
    G, D = N_KV_HEADS, LANES
    hg = N_Q_HEADS // G
    qw, kvw = N_Q_HEADS * D, G * D
    ret_w = w_ret_out.shape[1]
    dk = ret_w // N_RET_HEADS
    page_rows = cache_kv.shape[2]
    n_pages = page_table.shape[1]
    past_len = n_pages * page_rows
    lay = _proj_layout(ret_w, qw, kvw)

    l = 0
    w_in_parts = _prep_w_in(w_in[l], d_model, lay["width"])
    wts = (w_ret_out[l], w_nsa_out[l], w_out[l], norm2_g[l], w_up[l], w_down[l])
    cwk = _cmp_weights(cmp_pe_k[l], cmp_w1_k[l], cmp_b1_k[l], cmp_w2_k[l])
    cwv = _cmp_weights(cmp_pe_v[l], cmp_w1_v[l], cmp_b1_v[l], cmp_w2_v[l])

    xp = x_prompt.reshape(T, d_model)
    tm = _pick_tile(T, 1024)
    xn = _rmsnorm(xp, norm1_g[l], _pick_tile(T, 256))
    proj, proj_g, gates = _in_proj(xn, w_in_parts, tm, "p")
    qn, paged, win, kvb = _nsa_prep(proj, lay, q_norm_g[l], ks_norm_g[l], kw_norm_g[l], _pick_tile(T, 256))
    nhb = T // CMP_STRIDE
    half_blocks = (paged[:, :2 * kvw].astype(bf16).reshape(nhb, CMP_STRIDE, 2, G, D)
                   .transpose(2, 3, 0, 1, 4).reshape(2, G, nhb, CMP_STRIDE * D))
    kc = _compress_prompt(half_blocks[0], cwk, kc_norm_g[l], True)
    vc = _compress_prompt(half_blocks[1], cwv, kc_norm_g[l], False)
    y_nsa = _nsa_prompt(qn, gates, kc, vc, kvb)
    y_ret, s_prompt = _ret_prompt(proj, lay, T)
    y_prompt = _dense_tail(xp, y_ret, y_nsa, proj_g, wts, tm).reshape(B, T, d_model)
    new_kv_prompt = paged.reshape(1, B, T, 4, G, D)
    n_keep = min(WINDOW, T)
    new_win_prompt = win[T - n_keep:].reshape(1, B, n_keep, 2, G, D)
    new_ret_prompt = s_prompt.reshape(1, B, N_RET_HEADS, dk, dk)

    Ms = Bs * Ts
    xs = x_sample.reshape(Ms, d_model)
    tms = _pick_tile(Ms, 512)
    xn = _rmsnorm(xs, norm1_g[l], _pick_tile(Ms, 256))
    proj_s, proj_gs, gates = _in_proj(xn, w_in_parts, tms, "s")
    qn, paged_s, win_s, kvb = _nsa_prep(proj_s, lay, q_norm_g[l], ks_norm_g[l], kw_norm_g[l], _pick_tile(Ms, 256))
    P8 = 8
    assert Ts <= P8
    proj8 = jnp.pad(proj_s.reshape(Bs, Ts, lay["width"]), ((0, 0), (0, P8 - Ts), (0, 0)))
    y_ret8, s_sample = _ret_sample(proj8, lay, state_ret[l], past_len, Ts)
    y_ret = y_ret8[:, :Ts].reshape(Ms, ret_w)
    q_s = qn.reshape(Bs, Ts, G, hg, D).transpose(0, 2, 3, 1, 4).reshape(Bs, G, hg * Ts, D)
    gate_s = (gates.reshape(Bs, Ts, G, LANES)[..., :3 * hg]
              .reshape(Bs, Ts, G, hg, 3).transpose(0, 2, 3, 1, 4).reshape(Bs, G, hg * Ts, 3))
    gate_s = jnp.pad(gate_s, ((0, 0), (0, 0), (0, 0), (0, LANES - 3)))
    kvb5 = jnp.pad(kvb.reshape(Bs, Ts, 4, G, D), ((0, 0), (0, P8 - Ts), (0, 0), (0, 0), (0, 0)))
    new_rows = [kvb5[:, :, c].transpose(0, 2, 1, 3) for c in range(4)]
    win_new = win_s.reshape(Bs, Ts, 2, G, D)
    o_c, sel = _sample_cmp(cache_kv, l, page_table, q_s, cwk, cwv, kc_norm_g[l], Ts)
    o_s, new_win_sample = _sample_attn(cache_kv, l, page_table, q_s, gate_s, new_rows, win_new, state_win_kv,
                                       sel, o_c, Ts)
    y_nsa = o_s.reshape(Bs, G, hg, Ts, D).transpose(0, 3, 1, 2, 4).reshape(Ms, qw)
    y_sample = _dense_tail(xs, y_ret, y_nsa, proj_gs, wts, tms).reshape(Bs, Ts, d_model)
    new_kv_sample = paged_s.reshape(1, Bs, Ts, 4, G, D)
    new_ret_sample = s_sample.reshape(1, Bs, N_RET_HEADS, dk, dk)
    return (y_prompt, y_sample, new_kv_prompt, new_win_prompt, new_ret_prompt,
            new_kv_sample, new_win_sample, new_ret_sample)
```
